```python
import math
import jax, jax.numpy as jnp
from jax import lax
import numpy as np

D_MODEL = 1024
BATCH = 4
SEQ = 4096
DEPTH = 2

N_HEADS_A = 4
HEAD_DIM_A = 64
W_A = N_HEADS_A * 2 * HEAD_DIM_A
N_HEADS_B = 4
HEAD_DIM_B = D_MODEL // 8
W_B = N_HEADS_B * HEAD_DIM_B
CONV_W = 4
Q_BLOCK = 128
MLSTM_CHUNK = 64
EPS = 1e-6
N_IN = 4 * W_A + 5 * W_B + 2 * N_HEADS_B + 2 * D_MODEL

kernel_name = "hybrid_diffattn_mlstm_gated_merge"


def _rmsnorm(x, g):
    xf = x.astype(jnp.float32)
    y = xf * lax.rsqrt(jnp.mean(xf * xf, axis=-1, keepdims=True) + EPS)
    return (y * g.astype(jnp.float32)).astype(x.dtype)


def _split(t, sizes):
    idx = np.cumsum(sizes)[:-1].tolist()
    return jnp.split(t, idx, axis=-1)


def _causal_conv(x, w):
    S = x.shape[1]
    xp = jnp.pad(x, ((0, 0), (CONV_W - 1, 0), (0, 0)))
    y = xp[:, 0:S] * w[0]
    for k in range(1, CONV_W):
        y = y + xp[:, k:k + S] * w[k]
    return y


def _alibi_slopes(n_heads):
    return 2.0 ** (-8.0 * jnp.arange(1, n_heads + 1, dtype=jnp.float32) / n_heads)


def _diff_attention(q, k, v, lam):
    B, S, _ = q.shape
    H, d = N_HEADS_A, HEAD_DIM_A
    qh = q.reshape(B, S, H, 2, d).transpose(0, 2, 3, 1, 4)
    kh = k.reshape(B, S, H, 2, d).transpose(0, 2, 3, 1, 4)
    vh = v.reshape(B, S, H, 2 * d).transpose(0, 2, 1, 3)
    nb = S // Q_BLOCK
    q_blocks = qh.reshape(B, H, 2, nb, Q_BLOCK, d).transpose(3, 0, 1, 2, 4, 5)
    slopes = _alibi_slopes(H)
    s_pos = jnp.arange(S)
    scale = d ** -0.5

    def block(args):
        q_blk, i = args
        t_pos = i * Q_BLOCK + jnp.arange(Q_BLOCK)
        dist = (t_pos[:, None] - s_pos[None, :]).astype(jnp.float32)
        bias = jnp.where(dist >= 0, -slopes[:, None, None] * dist, -jnp.inf)
        scores = jnp.einsum('bhcqd,bhcsd->bhcqs', q_blk, kh,
                            preferred_element_type=jnp.float32) * scale + bias[None, :, None]
        p = jax.nn.softmax(scores, axis=-1)
        p = p[:, :, 0] - lam * p[:, :, 1]
        return jnp.einsum('bhqs,bhse->bhqe', p.astype(vh.dtype), vh)

    out = lax.map(block, (q_blocks, jnp.arange(nb)))
    return out.transpose(1, 0, 3, 2, 4).reshape(B, S, H, 2 * d)


def _mlstm(q, k, v, ig, fg):
    B, S, _ = q.shape
    H, d, L = N_HEADS_B, HEAD_DIM_B, MLSTM_CHUNK
    nc = S // L

    def heads(t):
        return t.astype(jnp.float32).reshape(B, nc, L, H, d).transpose(1, 0, 3, 2, 4)

    def gates(t):
        return t.astype(jnp.float32).reshape(B, nc, L, H).transpose(1, 0, 3, 2)

    qc, kc, vc = heads(q), heads(k) * (d ** -0.5), heads(v)
    igc = gates(ig)
    lfc = jax.nn.log_sigmoid(gates(fg))
    causal = jnp.tril(jnp.ones((L, L), dtype=bool))

    def body(carry, xs):
        C, n, m = carry
        qj, kj, vj, ij, lj = xs
        b = jnp.cumsum(lj, axis=-1)
        logD = b[..., :, None] - b[..., None, :] + ij[..., None, :]
        logD = jnp.where(causal, logD, -jnp.inf)
        inter = b + m[..., None]
        m_row = jnp.maximum(inter, jnp.max(logD, axis=-1))
        Dm = jnp.exp(logD - m_row[..., None])
        w_inter = jnp.exp(inter - m_row)
        Sm = jnp.einsum('bhjd,bhsd->bhjs', qj, kj) * Dm
        num = jnp.einsum('bhjs,bhse->bhje', Sm, vj) + w_inter[..., None] * jnp.einsum('bhjd,bhde->bhje', qj, C)
        den = jnp.sum(Sm, axis=-1) + w_inter * jnp.einsum('bhjd,bhd->bhj', qj, n)
        h = num / jnp.maximum(jnp.abs(den), jnp.exp(-m_row))[..., None]
        bL = b[..., -1]
        a = bL[..., None] - b + ij
        m_new = jnp.maximum(bL + m, jnp.max(a, axis=-1))
        decay = jnp.exp(bL + m - m_new)
        wk = jnp.exp(a - m_new[..., None])
        C_new = decay[..., None, None] * C + jnp.einsum('bhs,bhsd,bhse->bhde', wk, kj, vj)
        n_new = decay[..., None] * n + jnp.einsum('bhs,bhsd->bhd', wk, kj)
        return (C_new, n_new, m_new), h

    init = (jnp.zeros((B, H, d, d), jnp.float32), jnp.zeros((B, H, d), jnp.float32),
            jnp.zeros((B, H), jnp.float32))
    _, hs = lax.scan(body, init, (qc, kc, vc, igc, lfc))
    return hs.transpose(1, 0, 3, 2, 4).reshape(B, S, H, d)


def setup_inputs(seed: int = 0) -> dict:
    key = jax.random.key(seed)
    ks = jax.random.split(key, 14)
    nrm = jax.random.normal
    x = nrm(ks[0], (BATCH, SEQ, D_MODEL), jnp.float32)
    norm_pre = 1.0 + 0.05 * nrm(ks[1], (DEPTH, D_MODEL), jnp.float32)
    norm_post = 1.0 + 0.05 * nrm(ks[2], (DEPTH, D_MODEL), jnp.float32)
    w_in = nrm(ks[3], (DEPTH, D_MODEL, N_IN), jnp.float32) * D_MODEL ** -0.5
    b_i = 0.1 * nrm(ks[4], (DEPTH, N_HEADS_B), jnp.float32)
    b_f = jnp.linspace(3.0, 6.0, N_HEADS_B, dtype=jnp.float32)[None, :] + 0.01 * nrm(ks[5], (DEPTH, N_HEADS_B), jnp.float32)
    b_if = jnp.concatenate([b_i, b_f], axis=-1)
    conv_qk = nrm(ks[6], (DEPTH, CONV_W, 2 * W_B), jnp.float32) * CONV_W ** -0.5
    lambda_qk = 0.1 * nrm(ks[7], (DEPTH, 4, HEAD_DIM_A), jnp.float32)
    norm_a = 1.0 + 0.05 * nrm(ks[8], (DEPTH, W_A), jnp.float32)
    norm_b = 1.0 + 0.05 * nrm(ks[9], (DEPTH, W_B), jnp.float32)
    w_a = nrm(ks[10], (DEPTH, W_A, D_MODEL), jnp.float32) * W_A ** -0.5
    w_b = nrm(ks[11], (DEPTH, W_B, D_MODEL), jnp.float32) * W_B ** -0.5
    w_out = nrm(ks[12], (DEPTH, D_MODEL, D_MODEL), jnp.float32) * D_MODEL ** -0.5
    return {"x": x, "norm_pre": norm_pre, "norm_post": norm_post, "w_in": w_in,
            "b_if": b_if, "conv_qk": conv_qk, "lambda_qk": lambda_qk,
            "norm_a": norm_a, "norm_b": norm_b, "w_a": w_a, "w_b": w_b, "w_out": w_out}


def reference(x, norm_pre, norm_post, w_in, b_if, conv_qk, lambda_qk, norm_a, norm_b, w_a, w_b, w_out):
    B, S, _ = x.shape
    H_A, H_B = N_HEADS_A, N_HEADS_B
    sizes = [W_A, W_A, W_A, W_A, W_B, W_B, W_B, W_B, W_B, H_B, H_B, D_MODEL, D_MODEL]
    for l in range(DEPTH):
        h = _rmsnorm(x, norm_pre[l])
        proj = jnp.einsum('bsd,dn->bsn', h, w_in[l])
        qa, ka, va, za, qb, kb, vb, ob, zb, igb, fgb, ga, gb = _split(proj, sizes)

        lam_init = 0.8 - 0.6 * math.exp(-0.3 * l)
        lq = lambda_qk[l].astype(jnp.float32)
        lam = jnp.exp(jnp.sum(lq[0] * lq[1])) - jnp.exp(jnp.sum(lq[2] * lq[3])) + lam_init
        oa = _diff_attention(qa, ka, va, lam)
        oa = _rmsnorm(oa, norm_a[l].reshape(H_A, 2 * HEAD_DIM_A)) * (1.0 - lam_init)
        ya = oa.reshape(B, S, W_A) * jax.nn.silu(za)

        qk = jax.nn.silu(_causal_conv(jnp.concatenate([qb, kb], axis=-1), conv_qk[l]))
        qb_c, kb_c = qk[..., :W_B], qk[..., W_B:]
        ig = igb + b_if[l, :H_B]
        fg = fgb + b_if[l, H_B:]
        hb = _mlstm(qb_c, kb_c, vb, ig, fg)
        hb = jax.nn.sigmoid(ob.astype(jnp.float32)).reshape(B, S, H_B, HEAD_DIM_B) * hb
        hb = _rmsnorm(hb, norm_b[l].reshape(H_B, HEAD_DIM_B)).astype(x.dtype)
        yb = hb.reshape(B, S, W_B) * jax.nn.silu(zb)

        merged = (jax.nn.sigmoid(ga) * jnp.einsum('bsw,wd->bsd', ya, w_a[l])
                  + jax.nn.sigmoid(gb) * jnp.einsum('bsw,wd->bsd', yb, w_b[l]))
        out = jnp.einsum('bsd,de->bse', merged, w_out[l])
        x = x + _rmsnorm(out, norm_post[l])
    return x
```

```python
import functools
import math

import jax
import jax.numpy as jnp
import numpy as np
from jax import lax
from jax.experimental import pallas as pl
from jax.experimental.pallas import tpu as pltpu

F32 = jnp.float32
BF16 = jnp.bfloat16

N_HEADS_A = 4
HEAD_DIM_A = 64
N_HEADS_B = 4
HEAD_DIM_B = 128
CONV_W = 4
EPS = 1e-6

LANES = 128
SUBLANES = 8
NEG = -1e30
VMEM_LIMIT = 56 * 1024 * 1024


def _sigmoid(x):
    return jax.nn.sigmoid(x)


def _inproj_kernel(x_ref, g_ref, w_ref, wg_ref, p_ref, gate_ref, *, tn):
    x = x_ref[...]
    ms = jnp.mean(x * x, axis=-1, keepdims=True)
    h = (x * lax.rsqrt(ms + EPS) * g_ref[...]).astype(BF16)
    n_main = w_ref.shape[1]
    for c in range(n_main // tn):
        p_ref[:, c * tn:(c + 1) * tn] = jnp.dot(
            h, w_ref[:, c * tn:(c + 1) * tn], preferred_element_type=F32).astype(BF16)
    gate_ref[...] = jnp.dot(h, wg_ref[...], preferred_element_type=F32)


def _inproj(x2, g, w_main, w_gate, *, tm=512, tn=512):
    m, d = x2.shape
    n_main = w_main.shape[1]
    return pl.pallas_call(
        functools.partial(_inproj_kernel, tn=tn),
        grid=(m // tm,),
        in_specs=[
            pl.BlockSpec((tm, d), lambda i: (i, 0)),
            pl.BlockSpec((1, d), lambda i: (0, 0)),
            pl.BlockSpec((d, n_main), lambda i: (0, 0), pipeline_mode=pl.Buffered(1)),
            pl.BlockSpec((d, LANES), lambda i: (0, 0)),
        ],
        out_specs=[
            pl.BlockSpec((tm, n_main), lambda i: (i, 0)),
            pl.BlockSpec((tm, LANES), lambda i: (i, 0)),
        ],
        out_shape=[
            jax.ShapeDtypeStruct((m, n_main), BF16),
            jax.ShapeDtypeStruct((m, LANES), F32),
        ],
        compiler_params=pltpu.CompilerParams(
            dimension_semantics=("arbitrary",), vmem_limit_bytes=VMEM_LIMIT),
        name="inproj",
    )(x2, g, w_main, w_gate)


def _attn_kernel(slopes_ref, lq_ref, q_ref, k_ref, v_ref, za_ref, na_ref, o_ref,
                 m_sc, l_sc, acc_sc, *, tq, lam_init):
    h = pl.program_id(1)
    i = pl.program_id(2)
    d = HEAD_DIM_A
    slope = slopes_ref[h]

    q = q_ref[...] * jnp.asarray(d ** -0.5, BF16)
    lane = lax.broadcasted_iota(jnp.int32, q.shape, 1)
    zero = jnp.zeros_like(q)
    qs = jnp.concatenate([jnp.where(lane < d, q, zero), jnp.where(lane >= d, q, zero)], axis=0)

    m_sc[...] = jnp.full(m_sc.shape, NEG, F32)
    l_sc[...] = jnp.zeros(l_sc.shape, F32)
    acc_sc[...] = jnp.zeros(acc_sc.shape, F32)

    col = lax.broadcasted_iota(jnp.int32, (1, tq), 1)

    def block(j, masked):
        start = pl.multiple_of(j * tq, tq)
        k = k_ref[pl.ds(start, tq), :]
        v = v_ref[pl.ds(start, tq), :]
        s = lax.dot_general(qs, k, (((1,), (1,)), ((), ())), preferred_element_type=F32)
        s = s + (col + (j - i) * tq).astype(F32) * slope
        if masked:
            r = lax.broadcasted_iota(jnp.int32, (2 * tq, tq), 0)
            r = jnp.where(r >= tq, r - tq, r)
            c = lax.broadcasted_iota(jnp.int32, (2 * tq, tq), 1)
            s = jnp.where(c <= r, s, NEG)
        m_prev = m_sc[...]
        m_new = jnp.maximum(m_prev, jnp.max(s, axis=1, keepdims=True))
        alpha = jnp.exp(m_prev - m_new)
        p = jnp.exp(s - m_new)
        l_sc[...] = alpha * l_sc[...] + jnp.sum(p, axis=1, keepdims=True)
        acc_sc[...] = alpha * acc_sc[...] + jnp.dot(p.astype(BF16), v, preferred_element_type=F32)
        m_sc[...] = m_new

    def body(j, carry):
        block(j, False)
        return carry

    lax.fori_loop(0, i, body, 0)
    block(i, True)

    o = acc_sc[...] / l_sc[...]
    lq = lq_ref[...]
    lam = (jnp.exp(jnp.sum(lq[0:1] * lq[1:2], axis=1, keepdims=True))
           - jnp.exp(jnp.sum(lq[2:3] * lq[3:4], axis=1, keepdims=True)) + lam_init)
    od = o[:tq] - lam * o[tq:]
    ms = jnp.mean(od * od, axis=-1, keepdims=True)
    y = od * lax.rsqrt(ms + EPS) * na_ref[...] * (1.0 - lam_init)
    za = za_ref[...].astype(F32)
    o_ref[...] = (y * (za * _sigmoid(za))).astype(BF16)


def _attention(p3, slopes, lq, norm_a, *, lam_init, tq=512):
    b, s, _ = p3.shape
    h = N_HEADS_A
    blk = lambda base: (lambda bi, hi, qi: (bi, qi, base + hi))
    res = lambda base: (lambda bi, hi, qi: (bi, 0, base + hi))
    return pl.pallas_call(
        functools.partial(_attn_kernel, tq=tq, lam_init=lam_init),
        grid=(b, h, s // tq),
        in_specs=[
            pl.BlockSpec(memory_space=pltpu.SMEM),
            pl.BlockSpec((4, HEAD_DIM_A), lambda bi, hi, qi: (0, 0)),
            pl.BlockSpec((None, tq, LANES), blk(16)),
            pl.BlockSpec((None, s, LANES), res(20)),
            pl.BlockSpec((None, s, LANES), res(24)),
            pl.BlockSpec((None, tq, LANES), blk(28)),
            pl.BlockSpec((1, LANES), lambda bi, hi, qi: (0, hi)),
        ],
        out_specs=pl.BlockSpec((None, tq, LANES), lambda bi, hi, qi: (bi, qi, hi)),
        out_shape=jax.ShapeDtypeStruct((b, s, h * LANES), BF16),
        scratch_shapes=[
            pltpu.VMEM((2 * tq, 1), F32),
            pltpu.VMEM((2 * tq, 1), F32),
            pltpu.VMEM((2 * tq, LANES), F32),
        ],
        compiler_params=pltpu.CompilerParams(
            dimension_semantics=("arbitrary", "arbitrary", "arbitrary"),
            vmem_limit_bytes=VMEM_LIMIT),
        name="diffattn",
    )(slopes, lq, p3, p3, p3, p3, norm_a)


def _shift_rows(x, tail, sh):
    r = pltpu.roll(x, sh, 0)
    rt = pltpu.roll(tail, sh, 0)
    row = lax.broadcasted_iota(jnp.int32, tail.shape, 0)
    first = jnp.where(row < sh, rt, r[:SUBLANES])
    return jnp.concatenate([first, r[SUBLANES:]], axis=0)


def _conv_silu(x_ref, tail_sc, w_ref):
    x = x_ref[...].astype(F32)
    tail = tail_sc[...]
    w = w_ref[...]
    y = x * w[CONV_W - 1:CONV_W]
    for tap in range(CONV_W - 1):
        y = y + _shift_rows(x, tail, CONV_W - 1 - tap) * w[tap:tap + 1]
    tail_sc[...] = x[x.shape[0] - SUBLANES:]
    return y * _sigmoid(y)


def _split3(x):
    hi = x.astype(BF16)
    r1 = x - hi.astype(F32)
    mid = r1.astype(BF16)
    lo = (r1 - mid.astype(F32)).astype(BF16)
    return hi, mid, lo


def _mlstm_kernel(bias_ref, cq_ref, ck_ref, nb_ref, q_ref, k_ref, v_ref, o_ref, z_ref, g_ref,
                  y_ref, c_sc, n_sc, m_sc, tq_sc, tk_sc, wt_sc, *, chunk):
    L = chunk
    nh, dh = N_HEADS_B, HEAD_DIM_B

    @pl.when(pl.program_id(1) == 0)
    def _():
        c_sc[...] = jnp.zeros(c_sc.shape, F32)
        n_sc[...] = jnp.zeros(n_sc.shape, F32)
        m_sc[...] = jnp.zeros(m_sc.shape, F32)
        tq_sc[...] = jnp.zeros(tq_sc.shape, F32)
        tk_sc[...] = jnp.zeros(tk_sc.shape, F32)

    q_all = _conv_silu(q_ref, tq_sc, cq_ref)
    k_all = _conv_silu(k_ref, tk_sc, ck_ref) * (dh ** -0.5)

    g = g_ref[...] + bias_ref[...]
    lf = jnp.minimum(g, 0.0) - jnp.log1p(jnp.exp(-jnp.abs(g)))
    rr = lax.broadcasted_iota(jnp.int32, (L, L), 0)
    cc = lax.broadcasted_iota(jnp.int32, (L, L), 1)
    causal = cc <= rr
    tri = jnp.where(causal, 1.0, 0.0).astype(BF16)
    cs = sum(jnp.dot(tri, part, preferred_element_type=F32) for part in _split3(lf))
    lane = lax.broadcasted_iota(jnp.int32, (L, LANES), 1)
    w = jnp.where(lane < nh, g, cs)
    wt_sc[...] = w.T
    w_last = w[L - 1:L, :]
    lane1 = lax.broadcasted_iota(jnp.int32, (1, LANES), 1)

    def pick(x, ln, idx):
        return jnp.sum(jnp.where(ln == idx, x, 0.0), axis=1, keepdims=True)

    for h in range(nh):
        hs = slice(h * dh, (h + 1) * dh)
        b_col = pick(w, lane, nh + h)
        i_col = pick(w, lane, h)
        b_row = wt_sc[nh + h:nh + h + 1, :]
        i_row = wt_sc[h:h + 1, :]
        b_last = pick(w_last, lane1, nh + h)
        m_prev = jnp.max(m_sc[h:h + 1, :], axis=1, keepdims=True)

        log_d = jnp.where(causal, b_col + (i_row - b_row), NEG)
        inter = b_col + m_prev
        m_row = jnp.maximum(inter, jnp.max(log_d, axis=1, keepdims=True))
        d_m = jnp.exp(log_d - m_row)
        w_inter = jnp.exp(inter - m_row)

        q = q_all[:, hs]
        k = k_all[:, hs]
        qb = q.astype(BF16)
        v = v_ref[:, hs]
        s_m = lax.dot_general(qb, k.astype(BF16), (((1,), (1,)), ((), ())),
                              preferred_element_type=F32) * d_m
        c_old = c_sc[h]
        n_old = n_sc[h]
        num = (jnp.dot(s_m.astype(BF16), v, preferred_element_type=F32)
               + w_inter * jnp.dot(qb, c_old.astype(BF16), preferred_element_type=F32))
        den = (jnp.sum(s_m, axis=1, keepdims=True)
               + w_inter * jnp.sum(q * n_old, axis=1, keepdims=True))
        hval = num / jnp.maximum(jnp.abs(den), jnp.exp(-m_row))

        a_col = b_last - b_col + i_col
        m_new = jnp.maximum(b_last + m_prev, jnp.max(a_col, axis=0, keepdims=True))
        decay = jnp.exp(b_last + m_prev - m_new)
        kw = k * jnp.exp(a_col - m_new)
        c_sc[h] = decay * c_old + jnp.dot(kw.T.astype(BF16), v, preferred_element_type=F32)
        n_sc[h] = decay * n_old + jnp.sum(kw, axis=0, keepdims=True)
        m_sc[h:h + 1, :] = jnp.broadcast_to(m_new, (1, LANES))

        hb = _sigmoid(o_ref[:, hs].astype(F32)) * hval
        ms = jnp.mean(hb * hb, axis=-1, keepdims=True)
        yb = hb * lax.rsqrt(ms + EPS) * nb_ref[:, hs]
        z = z_ref[:, hs].astype(F32)
        y_ref[:, hs] = (yb * (z * _sigmoid(z))).astype(BF16)


def _mlstm(p3, gates3, bias_row, conv_q, conv_k, norm_b, *, chunk=256):
    b, s, _ = p3.shape
    wb = N_HEADS_B * HEAD_DIM_B
    col = lambda idx: (lambda bi, ci: (bi, ci, idx))
    const = lambda bi, ci: (0, 0)
    return pl.pallas_call(
        functools.partial(_mlstm_kernel, chunk=chunk),
        grid=(b, s // chunk),
        in_specs=[
            pl.BlockSpec((1, LANES), const),
            pl.BlockSpec((CONV_W, wb), const),
            pl.BlockSpec((CONV_W, wb), const),
            pl.BlockSpec((1, wb), const),
            pl.BlockSpec((None, chunk, wb), col(8)),
            pl.BlockSpec((None, chunk, wb), col(9)),
            pl.BlockSpec((None, chunk, wb), col(10)),
            pl.BlockSpec((None, chunk, wb), col(11)),
            pl.BlockSpec((None, chunk, wb), col(12)),
            pl.BlockSpec((None, chunk, LANES), lambda bi, ci: (bi, ci, 0)),
        ],
        out_specs=pl.BlockSpec((None, chunk, wb), lambda bi, ci: (bi, ci, 0)),
        out_shape=jax.ShapeDtypeStruct((b, s, wb), BF16),
        scratch_shapes=[
            pltpu.VMEM((N_HEADS_B, HEAD_DIM_B, HEAD_DIM_B), F32),
            pltpu.VMEM((N_HEADS_B, 1, HEAD_DIM_B), F32),
            pltpu.VMEM((SUBLANES, LANES), F32),
            pltpu.VMEM((SUBLANES, wb), F32),
            pltpu.VMEM((SUBLANES, wb), F32),
            pltpu.VMEM((LANES, chunk), F32),
        ],
        compiler_params=pltpu.CompilerParams(
            dimension_semantics=("arbitrary", "arbitrary"), vmem_limit_bytes=VMEM_LIMIT),
        name="mlstm",
    )(bias_row, conv_q, conv_k, norm_b, p3, p3, p3, p3, p3, gates3)


def _merge_kernel(x_ref, ya_ref, yb_ref, ga_ref, gb_ref, wa_ref, wb_ref, wo_ref, g_ref, o_ref):
    a = jnp.dot(ya_ref[...], wa_ref[...], preferred_element_type=F32)
    b = jnp.dot(yb_ref[...], wb_ref[...], preferred_element_type=F32)
    merged = _sigmoid(ga_ref[...].astype(F32)) * a + _sigmoid(gb_ref[...].astype(F32)) * b
    out = jnp.dot(merged.astype(BF16), wo_ref[...], preferred_element_type=F32)
    ms = jnp.mean(out * out, axis=-1, keepdims=True)
    o_ref[...] = x_ref[...] + out * lax.rsqrt(ms + EPS) * g_ref[...]


def _merge(x2, ya2, yb2, p2, w_a, w_b, w_out, g, *, tm=512):
    m, d = x2.shape
    wa = ya2.shape[1]
    row = lambda i: (i, 0)
    const = lambda i: (0, 0)
    return pl.pallas_call(
        _merge_kernel,
        grid=(m // tm,),
        in_specs=[
            pl.BlockSpec((tm, d), row),
            pl.BlockSpec((tm, wa), row),
            pl.BlockSpec((tm, wa), row),
            pl.BlockSpec((tm, d), lambda i: (i, 0)),
            pl.BlockSpec((tm, d), lambda i: (i, 1)),
            pl.BlockSpec((wa, d), const),
            pl.BlockSpec((wa, d), const),
            pl.BlockSpec((d, d), const),
            pl.BlockSpec((1, d), const),
        ],
        out_specs=pl.BlockSpec((tm, d), row),
        out_shape=jax.ShapeDtypeStruct((m, d), F32),
        compiler_params=pltpu.CompilerParams(
            dimension_semantics=("arbitrary",), vmem_limit_bytes=VMEM_LIMIT),
        name="merge",
    )(x2, ya2, yb2, p2, p2, w_a, w_b, w_out, g)


def kernel(x, norm_pre, norm_post, w_in, b_if, conv_qk, lambda_qk, norm_a, norm_b, w_a, w_b, w_out):
    bsz, seq, d = x.shape
    depth = w_in.shape[0]
    wa = N_HEADS_A * 2 * HEAD_DIM_A
    wb = N_HEADS_B * HEAD_DIM_B
    n9 = 4 * wa + 5 * wb
    ng = 2 * N_HEADS_B
    assert w_in.shape[2] == n9 + ng + 2 * d and d == 1024 and wa == 512 and wb == 512
    slopes = jnp.asarray(2.0 ** (-8.0 * np.arange(1, N_HEADS_A + 1) / N_HEADS_A), F32)

    x2 = x.reshape(bsz * seq, d)
    for l in range(depth):
        w = w_in[l]
        w_main = jnp.concatenate([w[:, n9 + ng:], w[:, :n9]], axis=1).astype(BF16)
        w_gate = jnp.pad(w[:, n9:n9 + ng], ((0, 0), (0, LANES - ng))).astype(BF16)
        bias_row = jnp.pad(b_if[l], (0, LANES - ng)).reshape(1, LANES)
        lam_init = 0.8 - 0.6 * math.exp(-0.3 * l)

        p2, gates2 = _inproj(x2, norm_pre[l].reshape(1, d), w_main, w_gate)
        p3 = p2.reshape(bsz, seq, -1)
        ya = _attention(p3, slopes, lambda_qk[l], norm_a[l].reshape(1, wa), lam_init=lam_init)
        yb = _mlstm(p3, gates2.reshape(bsz, seq, LANES), bias_row,
                    conv_qk[l][:, :wb], conv_qk[l][:, wb:], norm_b[l].reshape(1, wb))
        x2 = _merge(x2, ya.reshape(bsz * seq, wa), yb.reshape(bsz * seq, wb), p2,
                    w_a[l].astype(BF16), w_b[l].astype(BF16), w_out[l].astype(BF16),
                    norm_post[l].reshape(1, d))
    return x2.reshape(bsz, seq, d)
```

```python
import functools
import math

import jax
import jax.numpy as jnp
import numpy as np
from jax import lax
from jax.experimental import pallas as pl
from jax.experimental.pallas import tpu as pltpu

F32 = jnp.float32
BF16 = jnp.bfloat16

N_HEADS_A = 4
HEAD_DIM_A = 64
N_HEADS_B = 4
HEAD_DIM_B = 128
CONV_W = 4
EPS = 1e-6

LANES = 128
SUBLANES = 8
NEG = -1e30
VMEM_LIMIT = 56 * 1024 * 1024


def _sigmoid(x):
    return jax.nn.sigmoid(x)


def _inproj_kernel(x_ref, g_ref, w_ref, wg_ref, p_ref, gate_ref, *, tn):
    x = x_ref[...]
    ms = jnp.mean(x * x, axis=-1, keepdims=True)
    h = (x * lax.rsqrt(ms + EPS) * g_ref[...]).astype(BF16)
    n_main = w_ref.shape[1]
    for c in range(n_main // tn):
        p_ref[:, c * tn:(c + 1) * tn] = jnp.dot(
            h, w_ref[:, c * tn:(c + 1) * tn], preferred_element_type=F32).astype(BF16)
    gate_ref[...] = jnp.dot(h, wg_ref[...], preferred_element_type=F32)


def _inproj(x2, g, w_main, w_gate, *, tm=512, tn=512):
    m, d = x2.shape
    n_main = w_main.shape[1]
    return pl.pallas_call(
        functools.partial(_inproj_kernel, tn=tn),
        grid=(m // tm,),
        in_specs=[
            pl.BlockSpec((tm, d), lambda i: (i, 0)),
            pl.BlockSpec((1, d), lambda i: (0, 0)),
            pl.BlockSpec((d, n_main), lambda i: (0, 0), pipeline_mode=pl.Buffered(1)),
            pl.BlockSpec((d, LANES), lambda i: (0, 0)),
        ],
        out_specs=[
            pl.BlockSpec((tm, n_main), lambda i: (i, 0)),
            pl.BlockSpec((tm, LANES), lambda i: (i, 0)),
        ],
        out_shape=[
            jax.ShapeDtypeStruct((m, n_main), BF16),
            jax.ShapeDtypeStruct((m, LANES), F32),
        ],
        compiler_params=pltpu.CompilerParams(
            dimension_semantics=("arbitrary",), vmem_limit_bytes=VMEM_LIMIT),
        name="inproj",
    )(x2, g, w_main, w_gate)


def _attn_kernel(slopes_ref, lq_ref, q_ref, k_ref, v_ref, za_ref, na_ref, o_ref,
                 s_sc, m_sc, l_sc, acc_sc, *, tq, sub, lam_init):
    h = pl.program_id(1)
    i = pl.program_id(2)
    d = HEAD_DIM_A
    slope = slopes_ref[h]

    q = q_ref[...] * jnp.asarray(d ** -0.5, BF16)
    lane = lax.broadcasted_iota(jnp.int32, q.shape, 1)
    zero = jnp.zeros_like(q)
    qs = jnp.concatenate([jnp.where(lane < d, q, zero), jnp.where(lane >= d, q, zero)], axis=0)

    m_sc[...] = jnp.full(m_sc.shape, NEG, F32)
    l_sc[...] = jnp.zeros(l_sc.shape, F32)
    acc_sc[...] = jnp.zeros(acc_sc.shape, F32)

    col = lax.broadcasted_iota(jnp.int32, (1, tq), 1)
    rows2 = 2 * tq
    n_lt = tq // LANES

    def scores(j, slot):
        k = k_ref[pl.ds(pl.multiple_of(j * tq, tq), tq), :]
        s = lax.dot_general(qs, k, (((1,), (1,)), ((), ())), preferred_element_type=F32)
        s_sc[slot] = s + (col + (j - i) * tq).astype(F32) * slope

    def update(j, slot, masked):
        v = v_ref[pl.ds(pl.multiple_of(j * tq, tq), tq), :]
        ps, alphas = [], []
        for r0 in range(0, rows2, sub):
            rs = slice(r0, r0 + sub)
            s = s_sc[slot, rs, :]
            if masked:
                r = lax.broadcasted_iota(jnp.int32, (sub, tq), 0) + (r0 % tq)
                c = lax.broadcasted_iota(jnp.int32, (sub, tq), 1)
                s = jnp.where(c <= r, s, NEG)
            tiles = [s[:, t * LANES:(t + 1) * LANES] for t in range(n_lt)]
            m_prev = m_sc[rs, :]
            m_cur = jnp.max(functools.reduce(jnp.maximum, tiles), axis=1, keepdims=True)
            m_new = jnp.maximum(m_prev, m_cur)
            alpha = jnp.exp(m_prev - m_new)
            p_tiles = [jnp.exp(t - m_new) for t in tiles]
            l_sc[rs, :] = alpha * l_sc[rs, :] + functools.reduce(jnp.add, p_tiles)
            m_sc[rs, :] = m_new
            alphas.append(alpha)
            ps.append(jnp.concatenate(p_tiles, axis=1).astype(BF16))
        pv = jnp.dot(jnp.concatenate(ps, axis=0), v, preferred_element_type=F32)
        acc_sc[...] = jnp.concatenate(alphas, axis=0) * acc_sc[...] + pv

    scores(0, 0)
    pairs = i // 2

    def body(jj, carry):
        j = 2 * jj
        scores(j + 1, 1)
        update(j, 0, False)
        scores(j + 2, 0)
        update(j + 1, 1, False)
        return carry

    lax.fori_loop(0, pairs, body, 0)

    @pl.when(i % 2 == 0)
    def _():
        update(i, 0, True)

    @pl.when(i % 2 == 1)
    def _():
        scores(i, 1)
        update(i - 1, 0, False)
        update(i, 1, True)

    o = acc_sc[...] / jnp.sum(l_sc[...], axis=1, keepdims=True)
    lq = lq_ref[...]
    lam = (jnp.exp(jnp.sum(lq[0:1] * lq[1:2], axis=1, keepdims=True))
           - jnp.exp(jnp.sum(lq[2:3] * lq[3:4], axis=1, keepdims=True)) + lam_init)
    od = o[:tq] - lam * o[tq:]
    ms = jnp.mean(od * od, axis=-1, keepdims=True)
    y = od * lax.rsqrt(ms + EPS) * na_ref[...] * (1.0 - lam_init)
    za = za_ref[...].astype(F32)
    o_ref[...] = (y * (za * _sigmoid(za))).astype(BF16)


def _attention(p3, slopes, lq, norm_a, *, lam_init, tq=512, sub=128):
    b, s, _ = p3.shape
    h = N_HEADS_A
    blk = lambda base: (lambda bi, hi, qi: (bi, qi, base + hi))
    res = lambda base: (lambda bi, hi, qi: (bi, 0, base + hi))
    return pl.pallas_call(
        functools.partial(_attn_kernel, tq=tq, sub=sub, lam_init=lam_init),
        grid=(b, h, s // tq),
        in_specs=[
            pl.BlockSpec(memory_space=pltpu.SMEM),
            pl.BlockSpec((4, HEAD_DIM_A), lambda bi, hi, qi: (0, 0)),
            pl.BlockSpec((None, tq, LANES), blk(16)),
            pl.BlockSpec((None, s, LANES), res(20)),
            pl.BlockSpec((None, s, LANES), res(24)),
            pl.BlockSpec((None, tq, LANES), blk(28)),
            pl.BlockSpec((1, LANES), lambda bi, hi, qi: (0, hi)),
        ],
        out_specs=pl.BlockSpec((None, tq, LANES), lambda bi, hi, qi: (bi, qi, hi)),
        out_shape=jax.ShapeDtypeStruct((b, s, h * LANES), BF16),
        scratch_shapes=[
            pltpu.VMEM((2, 2 * tq, tq), F32),
            pltpu.VMEM((2 * tq, LANES), F32),
            pltpu.VMEM((2 * tq, LANES), F32),
            pltpu.VMEM((2 * tq, LANES), F32),
        ],
        compiler_params=pltpu.CompilerParams(
            dimension_semantics=("arbitrary", "arbitrary", "arbitrary"),
            vmem_limit_bytes=VMEM_LIMIT),
        name="diffattn",
    )(slopes, lq, p3, p3, p3, p3, norm_a)


def _shift_rows(x, tail, sh):
    r = pltpu.roll(x, sh, 0)
    rt = pltpu.roll(tail, sh, 0)
    row = lax.broadcasted_iota(jnp.int32, tail.shape, 0)
    first = jnp.where(row < sh, rt, r[:SUBLANES])
    return jnp.concatenate([first, r[SUBLANES:]], axis=0)


def _conv_silu(x_ref, tail_sc, w_ref):
    x = x_ref[...].astype(F32)
    tail = tail_sc[...]
    w = w_ref[...]
    y = x * w[CONV_W - 1:CONV_W]
    for tap in range(CONV_W - 1):
        y = y + _shift_rows(x, tail, CONV_W - 1 - tap) * w[tap:tap + 1]
    tail_sc[...] = x[x.shape[0] - SUBLANES:]
    return y * _sigmoid(y)


def _split3(x):
    hi = x.astype(BF16)
    r1 = x - hi.astype(F32)
    mid = r1.astype(BF16)
    lo = (r1 - mid.astype(F32)).astype(BF16)
    return hi, mid, lo


def _mlstm_kernel(bias_ref, cq_ref, ck_ref, nb_ref, q_ref, k_ref, v_ref, o_ref, z_ref, g_ref,
                  y_ref, c_sc, n_sc, m_sc, tq_sc, tk_sc, wt_sc, *, chunk):
    L = chunk
    nh, dh = N_HEADS_B, HEAD_DIM_B

    @pl.when(pl.program_id(1) == 0)
    def _():
        c_sc[...] = jnp.zeros(c_sc.shape, F32)
        n_sc[...] = jnp.zeros(n_sc.shape, F32)
        m_sc[...] = jnp.zeros(m_sc.shape, F32)
        tq_sc[...] = jnp.zeros(tq_sc.shape, F32)
        tk_sc[...] = jnp.zeros(tk_sc.shape, F32)

    q_all = _conv_silu(q_ref, tq_sc, cq_ref)
    k_all = _conv_silu(k_ref, tk_sc, ck_ref) * (dh ** -0.5)

    g = g_ref[...] + bias_ref[...]
    lf = jnp.minimum(g, 0.0) - jnp.log1p(jnp.exp(-jnp.abs(g)))
    rr = lax.broadcasted_iota(jnp.int32, (L, L), 0)
    cc = lax.broadcasted_iota(jnp.int32, (L, L), 1)
    causal = cc <= rr
    tri = jnp.where(causal, 1.0, 0.0).astype(BF16)
    cs = sum(jnp.dot(tri, part, preferred_element_type=F32) for part in _split3(lf))
    lane = lax.broadcasted_iota(jnp.int32, (L, LANES), 1)
    w = jnp.where(lane < nh, g, cs)
    wt_sc[...] = w.T
    w_last = w[L - 1:L, :]
    lane1 = lax.broadcasted_iota(jnp.int32, (1, LANES), 1)

    def pick(x, ln, idx):
        return jnp.sum(jnp.where(ln == idx, x, 0.0), axis=1, keepdims=True)

    for h in range(nh):
        hs = slice(h * dh, (h + 1) * dh)
        b_col = pick(w, lane, nh + h)
        i_col = pick(w, lane, h)
        b_row = wt_sc[nh + h:nh + h + 1, :]
        i_row = wt_sc[h:h + 1, :]
        b_last = pick(w_last, lane1, nh + h)
        m_prev = jnp.max(m_sc[h:h + 1, :], axis=1, keepdims=True)

        log_d = jnp.where(causal, b_col + (i_row - b_row), NEG)
        inter = b_col + m_prev
        m_row = jnp.maximum(inter, jnp.max(log_d, axis=1, keepdims=True))
        d_m = jnp.exp(log_d - m_row)
        w_inter = jnp.exp(inter - m_row)

        q = q_all[:, hs]
        k = k_all[:, hs]
        qb = q.astype(BF16)
        v = v_ref[:, hs]
        s_m = lax.dot_general(qb, k.astype(BF16), (((1,), (1,)), ((), ())),
                              preferred_element_type=F32) * d_m
        c_old = c_sc[h]
        n_old = n_sc[h]
        num = (jnp.dot(s_m.astype(BF16), v, preferred_element_type=F32)
               + w_inter * jnp.dot(qb, c_old.astype(BF16), preferred_element_type=F32))
        den = (jnp.sum(s_m, axis=1, keepdims=True)
               + w_inter * jnp.sum(q * n_old, axis=1, keepdims=True))
        hval = num / jnp.maximum(jnp.abs(den), jnp.exp(-m_row))

        a_col = b_last - b_col + i_col
        m_new = jnp.maximum(b_last + m_prev, jnp.max(a_col, axis=0, keepdims=True))
        decay = jnp.exp(b_last + m_prev - m_new)
        kw = k * jnp.exp(a_col - m_new)
        c_sc[h] = decay * c_old + jnp.dot(kw.T.astype(BF16), v, preferred_element_type=F32)
        n_sc[h] = decay * n_old + jnp.sum(kw, axis=0, keepdims=True)
        m_sc[h:h + 1, :] = jnp.broadcast_to(m_new, (1, LANES))

        hb = _sigmoid(o_ref[:, hs].astype(F32)) * hval
        ms = jnp.mean(hb * hb, axis=-1, keepdims=True)
        yb = hb * lax.rsqrt(ms + EPS) * nb_ref[:, hs]
        z = z_ref[:, hs].astype(F32)
        y_ref[:, hs] = (yb * (z * _sigmoid(z))).astype(BF16)


def _mlstm(p3, gates3, bias_row, conv_q, conv_k, norm_b, *, chunk=256):
    b, s, _ = p3.shape
    wb = N_HEADS_B * HEAD_DIM_B
    col = lambda idx: (lambda bi, ci: (bi, ci, idx))
    const = lambda bi, ci: (0, 0)
    return pl.pallas_call(
        functools.partial(_mlstm_kernel, chunk=chunk),
        grid=(b, s // chunk),
        in_specs=[
            pl.BlockSpec((1, LANES), const),
            pl.BlockSpec((CONV_W, wb), const),
            pl.BlockSpec((CONV_W, wb), const),
            pl.BlockSpec((1, wb), const),
            pl.BlockSpec((None, chunk, wb), col(8)),
            pl.BlockSpec((None, chunk, wb), col(9)),
            pl.BlockSpec((None, chunk, wb), col(10)),
            pl.BlockSpec((None, chunk, wb), col(11)),
            pl.BlockSpec((None, chunk, wb), col(12)),
            pl.BlockSpec((None, chunk, LANES), lambda bi, ci: (bi, ci, 0)),
        ],
        out_specs=pl.BlockSpec((None, chunk, wb), lambda bi, ci: (bi, ci, 0)),
        out_shape=jax.ShapeDtypeStruct((b, s, wb), BF16),
        scratch_shapes=[
            pltpu.VMEM((N_HEADS_B, HEAD_DIM_B, HEAD_DIM_B), F32),
            pltpu.VMEM((N_HEADS_B, 1, HEAD_DIM_B), F32),
            pltpu.VMEM((SUBLANES, LANES), F32),
            pltpu.VMEM((SUBLANES, wb), F32),
            pltpu.VMEM((SUBLANES, wb), F32),
            pltpu.VMEM((LANES, chunk), F32),
        ],
        compiler_params=pltpu.CompilerParams(
            dimension_semantics=("arbitrary", "arbitrary"), vmem_limit_bytes=VMEM_LIMIT),
        name="mlstm",
    )(bias_row, conv_q, conv_k, norm_b, p3, p3, p3, p3, p3, gates3)


def _merge_kernel(x_ref, ya_ref, yb_ref, ga_ref, gb_ref, wa_ref, wb_ref, wo_ref, g_ref, o_ref):
    a = jnp.dot(ya_ref[...], wa_ref[...], preferred_element_type=F32)
    b = jnp.dot(yb_ref[...], wb_ref[...], preferred_element_type=F32)
    merged = _sigmoid(ga_ref[...].astype(F32)) * a + _sigmoid(gb_ref[...].astype(F32)) * b
    out = jnp.dot(merged.astype(BF16), wo_ref[...], preferred_element_type=F32)
    ms = jnp.mean(out * out, axis=-1, keepdims=True)
    o_ref[...] = x_ref[...] + out * lax.rsqrt(ms + EPS) * g_ref[...]


def _merge(x2, ya2, yb2, p2, w_a, w_b, w_out, g, *, tm=512):
    m, d = x2.shape
    wa = ya2.shape[1]
    row = lambda i: (i, 0)
    const = lambda i: (0, 0)
    return pl.pallas_call(
        _merge_kernel,
        grid=(m // tm,),
        in_specs=[
            pl.BlockSpec((tm, d), row),
            pl.BlockSpec((tm, wa), row),
            pl.BlockSpec((tm, wa), row),
            pl.BlockSpec((tm, d), lambda i: (i, 0)),
            pl.BlockSpec((tm, d), lambda i: (i, 1)),
            pl.BlockSpec((wa, d), const),
            pl.BlockSpec((wa, d), const),
            pl.BlockSpec((d, d), const),
            pl.BlockSpec((1, d), const),
        ],
        out_specs=pl.BlockSpec((tm, d), row),
        out_shape=jax.ShapeDtypeStruct((m, d), F32),
        compiler_params=pltpu.CompilerParams(
            dimension_semantics=("arbitrary",), vmem_limit_bytes=VMEM_LIMIT),
        name="merge",
    )(x2, ya2, yb2, p2, p2, w_a, w_b, w_out, g)


def kernel(x, norm_pre, norm_post, w_in, b_if, conv_qk, lambda_qk, norm_a, norm_b, w_a, w_b, w_out):
    bsz, seq, d = x.shape
    depth = w_in.shape[0]
    wa = N_HEADS_A * 2 * HEAD_DIM_A
    wb = N_HEADS_B * HEAD_DIM_B
    n9 = 4 * wa + 5 * wb
    ng = 2 * N_HEADS_B
    assert w_in.shape[2] == n9 + ng + 2 * d and d == 1024 and wa == 512 and wb == 512
    slopes = jnp.asarray(2.0 ** (-8.0 * np.arange(1, N_HEADS_A + 1) / N_HEADS_A), F32)

    x2 = x.reshape(bsz * seq, d)
    for l in range(depth):
        w = w_in[l]
        w_main = jnp.concatenate([w[:, n9 + ng:], w[:, :n9]], axis=1).astype(BF16)
        w_gate = jnp.pad(w[:, n9:n9 + ng], ((0, 0), (0, LANES - ng))).astype(BF16)
        bias_row = jnp.pad(b_if[l], (0, LANES - ng)).reshape(1, LANES)
        lam_init = 0.8 - 0.6 * math.exp(-0.3 * l)

        p2, gates2 = _inproj(x2, norm_pre[l].reshape(1, d), w_main, w_gate)
        p3 = p2.reshape(bsz, seq, -1)
        ya = _attention(p3, slopes, lambda_qk[l], norm_a[l].reshape(1, wa), lam_init=lam_init)
        yb = _mlstm(p3, gates2.reshape(bsz, seq, LANES), bias_row,
                    conv_qk[l][:, :wb], conv_qk[l][:, wb:], norm_b[l].reshape(1, wb))
        x2 = _merge(x2, ya.reshape(bsz * seq, wa), yb.reshape(bsz * seq, wb), p2,
                    w_a[l].astype(BF16), w_b[l].astype(BF16), w_out[l].astype(BF16),
                    norm_post[l].reshape(1, d))
    return x2.reshape(bsz, seq, d)
```

```python
import functools
import math

import jax
import jax.numpy as jnp
import numpy as np
from jax import lax
from jax.experimental import pallas as pl
from jax.experimental.pallas import tpu as pltpu

F32 = jnp.float32
BF16 = jnp.bfloat16

N_HEADS_A = 4
HEAD_DIM_A = 64
N_HEADS_B = 4
HEAD_DIM_B = 128
CONV_W = 4
EPS = 1e-6

LANES = 128
SUBLANES = 8
NEG = -1e30
LOG2E = math.log2(math.e)
VMEM_LIMIT = 56 * 1024 * 1024


def _sigmoid(x):
    return jax.nn.sigmoid(x)


def _split3(x):
    hi = x.astype(BF16)
    r1 = x - hi.astype(F32)
    mid = r1.astype(BF16)
    lo = (r1 - mid.astype(F32)).astype(BF16)
    return hi, mid, lo


def _inproj_kernel(x_ref, g_ref, w_ref, wg_ref, wvt_ref, p_ref, gate_ref, vt_ref, *, tn):
    x = x_ref[...]
    ms = jnp.mean(x * x, axis=-1, keepdims=True)
    h = (x * lax.rsqrt(ms + EPS) * g_ref[...]).astype(BF16)
    n_main = w_ref.shape[1]
    for c in range(n_main // tn):
        p_ref[:, c * tn:(c + 1) * tn] = jnp.dot(
            h, w_ref[:, c * tn:(c + 1) * tn], preferred_element_type=F32).astype(BF16)
    gate_ref[...] = jnp.dot(h, wg_ref[...], preferred_element_type=F32)
    vt_ref[...] = lax.dot_general(wvt_ref[...], h, (((1,), (1,)), ((), ())),
                                  preferred_element_type=F32).astype(BF16)


def _inproj(x2, g, w_main, w_gate, w_vt, *, seq, tm=512, tn=512):
    m, d = x2.shape
    n_main = w_main.shape[1]
    wv = w_vt.shape[0]
    nt = seq // tm
    return pl.pallas_call(
        functools.partial(_inproj_kernel, tn=tn),
        grid=(m // tm,),
        in_specs=[
            pl.BlockSpec((tm, d), lambda i: (i, 0)),
            pl.BlockSpec((1, d), lambda i: (0, 0)),
            pl.BlockSpec((d, n_main), lambda i: (0, 0), pipeline_mode=pl.Buffered(1)),
            pl.BlockSpec((d, LANES), lambda i: (0, 0)),
            pl.BlockSpec((wv, d), lambda i: (0, 0)),
        ],
        out_specs=[
            pl.BlockSpec((tm, n_main), lambda i: (i, 0)),
            pl.BlockSpec((tm, LANES), lambda i: (i, 0)),
            pl.BlockSpec((None, None, wv, tm), lambda i: (i // nt, i % nt, 0, 0)),
        ],
        out_shape=[
            jax.ShapeDtypeStruct((m, n_main), BF16),
            jax.ShapeDtypeStruct((m, LANES), F32),
            jax.ShapeDtypeStruct((m // seq, nt, wv, tm), BF16),
        ],
        compiler_params=pltpu.CompilerParams(
            dimension_semantics=("arbitrary",), vmem_limit_bytes=VMEM_LIMIT),
        name="inproj",
    )(x2, g, w_main, w_gate, w_vt)


def _attn_kernel(slopes_ref, lq_ref, q_ref, k_ref, vt_ref, za_ref, na_ref, o_ref,
                 s_sc, m_sc, acc_sc, *, tq, lam_init):
    h = pl.program_id(1)
    i = pl.program_id(2)
    d = HEAD_DIM_A
    slope2 = slopes_ref[h] * LOG2E
    rows2 = 2 * tq
    n_ct = rows2 // LANES

    q = (q_ref[...].astype(F32) * (d ** -0.5 * LOG2E)).astype(BF16)
    lane = lax.broadcasted_iota(jnp.int32, q.shape, 1)
    zero = jnp.zeros_like(q)
    qs = jnp.concatenate([jnp.where(lane < d, q, zero), jnp.where(lane >= d, q, zero)], axis=0)
    lane2 = lax.broadcasted_iota(jnp.int32, (rows2, LANES), 1)
    qs_aug = jnp.concatenate([qs, jnp.where(lane2 < 3, 1.0, 0.0).astype(BF16)], axis=1)
    kpos = lax.broadcasted_iota(jnp.int32, (tq, LANES), 0)
    klane = lax.broadcasted_iota(jnp.int32, (tq, LANES), 1)
    b_hi, b_mid, b_lo = (part.astype(F32) for part in _split3(kpos.astype(F32) * slope2))
    k_bias = jnp.where(klane == 0, b_hi, jnp.where(klane == 1, b_mid, jnp.where(klane == 2, b_lo, 0.0)))
    k_bias = k_bias.astype(BF16)
    ones_row = jnp.where(lax.broadcasted_iota(jnp.int32, (2 * SUBLANES, tq), 0) == 0, 1.0, 0.0).astype(BF16)

    m_sc[...] = jnp.full(m_sc.shape, NEG, F32)
    acc_sc[...] = jnp.zeros(acc_sc.shape, F32)

    def scores(j, slot):
        k = k_ref[pl.ds(pl.multiple_of(j * tq, tq), tq), :]
        s_sc[slot] = lax.dot_general(jnp.concatenate([k, k_bias], axis=1), qs_aug,
                                     (((1,), (1,)), ((), ())), preferred_element_type=F32)

    def update(j, slot, masked):
        off = ((j - i) * tq).astype(F32) * slope2
        ps, alphas = [], []
        for c in range(n_ct):
            cs = slice(c * LANES, (c + 1) * LANES)
            s = s_sc[slot, :, cs]
            if masked:
                qpos = lax.broadcasted_iota(jnp.int32, (tq, LANES), 1) + (c * LANES) % tq
                s = jnp.where(kpos <= qpos, s, NEG)
            m_prev = m_sc[:, cs]
            m_new = jnp.maximum(m_prev, jnp.max(s, axis=0, keepdims=True) + off)
            alphas.append(jnp.exp2(m_prev - m_new))
            ps.append(jnp.exp2(s - (m_new - off)).astype(BF16))
            m_sc[:, cs] = m_new
        vt_aug = jnp.concatenate([vt_ref[j], ones_row], axis=0)
        pv = jnp.dot(vt_aug, jnp.concatenate(ps, axis=1), preferred_element_type=F32)
        acc_sc[...] = jnp.concatenate(alphas, axis=1) * acc_sc[...] + pv

    scores(0, 0)
    pairs = i // 2

    def body(jj, carry):
        j = 2 * jj
        scores(j + 1, 1)
        update(j, 0, False)
        scores(j + 2, 0)
        update(j + 1, 1, False)
        return carry

    lax.fori_loop(0, pairs, body, 0)

    @pl.when(i % 2 == 0)
    def _():
        update(i, 0, True)

    @pl.when(i % 2 == 1)
    def _():
        scores(i, 1)
        update(i - 1, 0, False)
        update(i, 1, True)

    dv = 2 * d
    ot = acc_sc[:dv, :] / acc_sc[dv:dv + 1, :]
    lq = lq_ref[...]
    lam = (jnp.exp(jnp.sum(lq[0:1] * lq[1:2], axis=1, keepdims=True))
           - jnp.exp(jnp.sum(lq[2:3] * lq[3:4], axis=1, keepdims=True)) + lam_init)
    od = (ot[:, :tq] - lam * ot[:, tq:]).T
    ms = jnp.mean(od * od, axis=-1, keepdims=True)
    y = od * lax.rsqrt(ms + EPS) * na_ref[...] * (1.0 - lam_init)
    za = za_ref[...].astype(F32)
    o_ref[...] = (y * (za * _sigmoid(za))).astype(BF16)


def _attention(p3, vt4, slopes, lq, norm_a, *, lam_init, tq=512):
    b, s, _ = p3.shape
    h = N_HEADS_A
    nt = vt4.shape[1]
    assert vt4.shape[3] == tq
    blk = lambda base: (lambda bi, hi, qi: (bi, qi, base + hi))
    res = lambda base: (lambda bi, hi, qi: (bi, 0, base + hi))
    return pl.pallas_call(
        functools.partial(_attn_kernel, tq=tq, lam_init=lam_init),
        grid=(b, h, s // tq),
        in_specs=[
            pl.BlockSpec(memory_space=pltpu.SMEM),
            pl.BlockSpec((4, HEAD_DIM_A), lambda bi, hi, qi: (0, 0)),
            pl.BlockSpec((None, tq, LANES), blk(16)),
            pl.BlockSpec((None, s, LANES), res(20)),
            pl.BlockSpec((None, nt, LANES, tq), lambda bi, hi, qi: (bi, 0, hi, 0)),
            pl.BlockSpec((None, tq, LANES), blk(24)),
            pl.BlockSpec((1, LANES), lambda bi, hi, qi: (0, hi)),
        ],
        out_specs=pl.BlockSpec((None, tq, LANES), lambda bi, hi, qi: (bi, qi, hi)),
        out_shape=jax.ShapeDtypeStruct((b, s, h * LANES), BF16),
        scratch_shapes=[
            pltpu.VMEM((2, tq, 2 * tq), F32),
            pltpu.VMEM((1, 2 * tq), F32),
            pltpu.VMEM((LANES + 2 * SUBLANES, 2 * tq), F32),
        ],
        compiler_params=pltpu.CompilerParams(
            dimension_semantics=("arbitrary", "arbitrary", "arbitrary"),
            vmem_limit_bytes=VMEM_LIMIT),
        name="diffattn",
    )(slopes, lq, p3, p3, vt4, p3, norm_a)


def _shift_rows(x, tail, sh):
    r = pltpu.roll(x, sh, 0)
    rt = pltpu.roll(tail, sh, 0)
    row = lax.broadcasted_iota(jnp.int32, tail.shape, 0)
    first = jnp.where(row < sh, rt, r[:SUBLANES])
    return jnp.concatenate([first, r[SUBLANES:]], axis=0)


def _conv_silu(x_ref, tail_sc, w_ref):
    x = x_ref[...].astype(F32)
    tail = tail_sc[...]
    w = w_ref[...]
    y = x * w[CONV_W - 1:CONV_W]
    for tap in range(CONV_W - 1):
        y = y + _shift_rows(x, tail, CONV_W - 1 - tap) * w[tap:tap + 1]
    tail_sc[...] = x[x.shape[0] - SUBLANES:]
    return y * _sigmoid(y)


def _mlstm_kernel(bias_ref, cq_ref, ck_ref, nb_ref, q_ref, k_ref, v_ref, o_ref, z_ref, g_ref,
                  y_ref, c_sc, n_sc, m_sc, tq_sc, tk_sc, wt_sc, *, chunk):
    L = chunk
    nh, dh = N_HEADS_B, HEAD_DIM_B

    @pl.when(pl.program_id(1) == 0)
    def _():
        c_sc[...] = jnp.zeros(c_sc.shape, F32)
        n_sc[...] = jnp.zeros(n_sc.shape, F32)
        m_sc[...] = jnp.zeros(m_sc.shape, F32)
        tq_sc[...] = jnp.zeros(tq_sc.shape, F32)
        tk_sc[...] = jnp.zeros(tk_sc.shape, F32)

    q_all = _conv_silu(q_ref, tq_sc, cq_ref)
    k_all = _conv_silu(k_ref, tk_sc, ck_ref) * (dh ** -0.5)

    g = g_ref[...] + bias_ref[...]
    lf = jnp.minimum(g, 0.0) - jnp.log1p(jnp.exp(-jnp.abs(g)))
    rr = lax.broadcasted_iota(jnp.int32, (L, L), 0)
    cc = lax.broadcasted_iota(jnp.int32, (L, L), 1)
    causal = cc <= rr
    tri = jnp.where(causal, 1.0, 0.0).astype(BF16)
    cs = sum(jnp.dot(tri, part, preferred_element_type=F32) for part in _split3(lf))
    lane = lax.broadcasted_iota(jnp.int32, (L, LANES), 1)
    w = jnp.where(lane < nh, g, cs)
    wt_sc[...] = w.T
    w_last = w[L - 1:L, :]
    lane1 = lax.broadcasted_iota(jnp.int32, (1, LANES), 1)

    def pick(x, ln, idx):
        return jnp.sum(jnp.where(ln == idx, x, 0.0), axis=1, keepdims=True)

    for h in range(nh):
        hs = slice(h * dh, (h + 1) * dh)
        b_col = pick(w, lane, nh + h)
        i_col = pick(w, lane, h)
        b_row = wt_sc[nh + h:nh + h + 1, :]
        i_row = wt_sc[h:h + 1, :]
        b_last = pick(w_last, lane1, nh + h)
        m_prev = jnp.max(m_sc[h:h + 1, :], axis=1, keepdims=True)

        log_d = jnp.where(causal, b_col + (i_row - b_row), NEG)
        inter = b_col + m_prev
        m_row = jnp.maximum(inter, jnp.max(log_d, axis=1, keepdims=True))
        d_m = jnp.exp(log_d - m_row)
        w_inter = jnp.exp(inter - m_row)

        q = q_all[:, hs]
        k = k_all[:, hs]
        qb = q.astype(BF16)
        v = v_ref[:, hs]
        s_m = lax.dot_general(qb, k.astype(BF16), (((1,), (1,)), ((), ())),
                              preferred_element_type=F32) * d_m
        c_old = c_sc[h]
        n_old = n_sc[h]
        num = (jnp.dot(s_m.astype(BF16), v, preferred_element_type=F32)
               + w_inter * jnp.dot(qb, c_old.astype(BF16), preferred_element_type=F32))
        den = (jnp.sum(s_m, axis=1, keepdims=True)
               + w_inter * jnp.sum(q * n_old, axis=1, keepdims=True))
        hval = num / jnp.maximum(jnp.abs(den), jnp.exp(-m_row))

        a_col = b_last - b_col + i_col
        m_new = jnp.maximum(b_last + m_prev, jnp.max(a_col, axis=0, keepdims=True))
        decay = jnp.exp(b_last + m_prev - m_new)
        kw = k * jnp.exp(a_col - m_new)
        c_sc[h] = decay * c_old + jnp.dot(kw.T.astype(BF16), v, preferred_element_type=F32)
        n_sc[h] = decay * n_old + jnp.sum(kw, axis=0, keepdims=True)
        m_sc[h:h + 1, :] = jnp.broadcast_to(m_new, (1, LANES))

        hb = _sigmoid(o_ref[:, hs].astype(F32)) * hval
        ms = jnp.mean(hb * hb, axis=-1, keepdims=True)
        yb = hb * lax.rsqrt(ms + EPS) * nb_ref[:, hs]
        z = z_ref[:, hs].astype(F32)
        y_ref[:, hs] = (yb * (z * _sigmoid(z))).astype(BF16)


def _mlstm(p3, gates3, bias_row, conv_q, conv_k, norm_b, *, chunk=256):
    b, s, _ = p3.shape
    wb = N_HEADS_B * HEAD_DIM_B
    col = lambda idx: (lambda bi, ci: (bi, ci, idx))
    const = lambda bi, ci: (0, 0)
    return pl.pallas_call(
        functools.partial(_mlstm_kernel, chunk=chunk),
        grid=(b, s // chunk),
        in_specs=[
            pl.BlockSpec((1, LANES), const),
            pl.BlockSpec((CONV_W, wb), const),
            pl.BlockSpec((CONV_W, wb), const),
            pl.BlockSpec((1, wb), const),
            pl.BlockSpec((None, chunk, wb), col(7)),
            pl.BlockSpec((None, chunk, wb), col(8)),
            pl.BlockSpec((None, chunk, wb), col(9)),
            pl.BlockSpec((None, chunk, wb), col(10)),
            pl.BlockSpec((None, chunk, wb), col(11)),
            pl.BlockSpec((None, chunk, LANES), lambda bi, ci: (bi, ci, 0)),
        ],
        out_specs=pl.BlockSpec((None, chunk, wb), lambda bi, ci: (bi, ci, 0)),
        out_shape=jax.ShapeDtypeStruct((b, s, wb), BF16),
        scratch_shapes=[
            pltpu.VMEM((N_HEADS_B, HEAD_DIM_B, HEAD_DIM_B), F32),
            pltpu.VMEM((N_HEADS_B, 1, HEAD_DIM_B), F32),
            pltpu.VMEM((SUBLANES, LANES), F32),
            pltpu.VMEM((SUBLANES, wb), F32),
            pltpu.VMEM((SUBLANES, wb), F32),
            pltpu.VMEM((LANES, chunk), F32),
        ],
        compiler_params=pltpu.CompilerParams(
            dimension_semantics=("arbitrary", "arbitrary"), vmem_limit_bytes=VMEM_LIMIT),
        name="mlstm",
    )(bias_row, conv_q, conv_k, norm_b, p3, p3, p3, p3, p3, gates3)


def _merge_kernel(x_ref, ya_ref, yb_ref, ga_ref, gb_ref, wa_ref, wb_ref, wo_ref, g_ref, o_ref):
    a = jnp.dot(ya_ref[...], wa_ref[...], preferred_element_type=F32)
    b = jnp.dot(yb_ref[...], wb_ref[...], preferred_element_type=F32)
    merged = _sigmoid(ga_ref[...].astype(F32)) * a + _sigmoid(gb_ref[...].astype(F32)) * b
    out = jnp.dot(merged.astype(BF16), wo_ref[...], preferred_element_type=F32)
    ms = jnp.mean(out * out, axis=-1, keepdims=True)
    o_ref[...] = x_ref[...] + out * lax.rsqrt(ms + EPS) * g_ref[...]


def _merge(x2, ya2, yb2, p2, w_a, w_b, w_out, g, *, tm=512):
    m, d = x2.shape
    wa = ya2.shape[1]
    row = lambda i: (i, 0)
    const = lambda i: (0, 0)
    return pl.pallas_call(
        _merge_kernel,
        grid=(m // tm,),
        in_specs=[
            pl.BlockSpec((tm, d), row),
            pl.BlockSpec((tm, wa), row),
            pl.BlockSpec((tm, wa), row),
            pl.BlockSpec((tm, d), lambda i: (i, 0)),
            pl.BlockSpec((tm, d), lambda i: (i, 1)),
            pl.BlockSpec((wa, d), const),
            pl.BlockSpec((wa, d), const),
            pl.BlockSpec((d, d), const),
            pl.BlockSpec((1, d), const),
        ],
        out_specs=pl.BlockSpec((tm, d), row),
        out_shape=jax.ShapeDtypeStruct((m, d), F32),
        compiler_params=pltpu.CompilerParams(
            dimension_semantics=("arbitrary",), vmem_limit_bytes=VMEM_LIMIT),
        name="merge",
    )(x2, ya2, yb2, p2, p2, w_a, w_b, w_out, g)


def kernel(x, norm_pre, norm_post, w_in, b_if, conv_qk, lambda_qk, norm_a, norm_b, w_a, w_b, w_out):
    bsz, seq, d = x.shape
    depth = w_in.shape[0]
    wa = N_HEADS_A * 2 * HEAD_DIM_A
    wb = N_HEADS_B * HEAD_DIM_B
    n9 = 4 * wa + 5 * wb
    ng = 2 * N_HEADS_B
    assert w_in.shape[2] == n9 + ng + 2 * d and d == 1024 and wa == 512 and wb == 512
    slopes = jnp.asarray(2.0 ** (-8.0 * np.arange(1, N_HEADS_A + 1) / N_HEADS_A), F32)

    x2 = x.reshape(bsz * seq, d)
    for l in range(depth):
        w = w_in[l]
        w_main = jnp.concatenate(
            [w[:, n9 + ng:], w[:, :2 * wa], w[:, 3 * wa:n9]], axis=1).astype(BF16)
        w_vt = w[:, 2 * wa:3 * wa].T.astype(BF16)
        w_gate = jnp.pad(w[:, n9:n9 + ng], ((0, 0), (0, LANES - ng))).astype(BF16)
        bias_row = jnp.pad(b_if[l], (0, LANES - ng)).reshape(1, LANES)
        lam_init = 0.8 - 0.6 * math.exp(-0.3 * l)

        p2, gates2, vt4 = _inproj(x2, norm_pre[l].reshape(1, d), w_main, w_gate, w_vt, seq=seq)
        p3 = p2.reshape(bsz, seq, -1)
        ya = _attention(p3, vt4, slopes, lambda_qk[l], norm_a[l].reshape(1, wa), lam_init=lam_init)
        yb = _mlstm(p3, gates2.reshape(bsz, seq, LANES), bias_row,
                    conv_qk[l][:, :wb], conv_qk[l][:, wb:], norm_b[l].reshape(1, wb))
        x2 = _merge(x2, ya.reshape(bsz * seq, wa), yb.reshape(bsz * seq, wb), p2,
                    w_a[l].astype(BF16), w_b[l].astype(BF16), w_out[l].astype(BF16),
                    norm_post[l].reshape(1, d))
    return x2.reshape(bsz, seq, d)
```

```python
import functools
import math

import jax
import jax.numpy as jnp
import numpy as np
from jax import lax
from jax.experimental import pallas as pl
from jax.experimental.pallas import tpu as pltpu

F32 = jnp.float32
BF16 = jnp.bfloat16

N_HEADS_A = 4
HEAD_DIM_A = 64
N_HEADS_B = 4
HEAD_DIM_B = 128
CONV_W = 4
EPS = 1e-6

LANES = 128
SUBLANES = 8
NEG = -1e30
LOG2E = math.log2(math.e)
VMEM_LIMIT = 56 * 1024 * 1024


def _sigmoid(x):
    return jax.nn.sigmoid(x)


def _split3(x):
    hi = x.astype(BF16)
    r1 = x - hi.astype(F32)
    mid = r1.astype(BF16)
    lo = (r1 - mid.astype(F32)).astype(BF16)
    return hi, mid, lo


def _inproj_kernel(x_ref, g_ref, w_ref, wg_ref, wvt_ref, p_ref, gate_ref, vt_ref, *, tn):
    x = x_ref[...]
    ms = jnp.mean(x * x, axis=-1, keepdims=True)
    h = (x * lax.rsqrt(ms + EPS) * g_ref[...]).astype(BF16)
    n_main = w_ref.shape[1]
    for c in range(n_main // tn):
        p_ref[:, c * tn:(c + 1) * tn] = jnp.dot(
            h, w_ref[:, c * tn:(c + 1) * tn], preferred_element_type=F32).astype(BF16)
    gate_ref[...] = jnp.dot(h, wg_ref[...], preferred_element_type=F32)
    vt_ref[...] = lax.dot_general(wvt_ref[...], h, (((1,), (1,)), ((), ())),
                                  preferred_element_type=F32).astype(BF16)


def _inproj(x2, g, w_main, w_gate, w_vt, *, seq, tm=512, tn=512):
    m, d = x2.shape
    n_main = w_main.shape[1]
    wv = w_vt.shape[0]
    nt = seq // tm
    return pl.pallas_call(
        functools.partial(_inproj_kernel, tn=tn),
        grid=(m // tm,),
        in_specs=[
            pl.BlockSpec((tm, d), lambda i: (i, 0)),
            pl.BlockSpec((1, d), lambda i: (0, 0)),
            pl.BlockSpec((d, n_main), lambda i: (0, 0), pipeline_mode=pl.Buffered(1)),
            pl.BlockSpec((d, LANES), lambda i: (0, 0)),
            pl.BlockSpec((wv, d), lambda i: (0, 0)),
        ],
        out_specs=[
            pl.BlockSpec((tm, n_main), lambda i: (i, 0)),
            pl.BlockSpec((tm, LANES), lambda i: (i, 0)),
            pl.BlockSpec((None, None, wv, tm), lambda i: (i // nt, i % nt, 0, 0)),
        ],
        out_shape=[
            jax.ShapeDtypeStruct((m, n_main), BF16),
            jax.ShapeDtypeStruct((m, LANES), F32),
            jax.ShapeDtypeStruct((m // seq, nt, wv, tm), BF16),
        ],
        compiler_params=pltpu.CompilerParams(
            dimension_semantics=("arbitrary",), vmem_limit_bytes=VMEM_LIMIT),
        name="inproj",
    )(x2, g, w_main, w_gate, w_vt)


def _attn_kernel(slopes_ref, qi_ref, kj_ref, lq_ref, q_ref, k_ref, vt_ref, za_ref, na_ref, o_ref,
                 qs_sc, s_sc, m_sc, acc_sc, *, tq, nq, lam_init):
    h = pl.program_id(1)
    d = HEAD_DIM_A
    dv = 2 * d
    slope2 = slopes_ref[h] * LOG2E
    rows2 = 2 * tq
    n_ct = rows2 // LANES
    n_steps = nq * (nq + 1) // 2

    lane = lax.broadcasted_iota(jnp.int32, (tq, LANES), 1)
    lane2 = lax.broadcasted_iota(jnp.int32, (rows2, LANES), 1)
    ones_cols = jnp.where(lane2 < 3, 1.0, 0.0).astype(BF16)
    for qb in range(nq):
        q = (q_ref[qb * tq:(qb + 1) * tq, :].astype(F32) * (d ** -0.5 * LOG2E)).astype(BF16)
        zero = jnp.zeros_like(q)
        qs = jnp.concatenate([jnp.where(lane < d, q, zero), jnp.where(lane >= d, q, zero)], axis=0)
        qs_sc[qb] = jnp.concatenate([qs, ones_cols], axis=1)
    kpos = lax.broadcasted_iota(jnp.int32, (tq, LANES), 0)
    b_hi, b_mid, b_lo = (part.astype(F32) for part in _split3(kpos.astype(F32) * slope2))
    k_bias = jnp.where(lane == 0, b_hi, jnp.where(lane == 1, b_mid, jnp.where(lane == 2, b_lo, 0.0)))
    k_bias = k_bias.astype(BF16)
    ones_row = jnp.where(lax.broadcasted_iota(jnp.int32, (2 * SUBLANES, tq), 0) == 0, 1.0, 0.0).astype(BF16)

    lq = lq_ref[...]
    lam = (jnp.exp(jnp.sum(lq[0:1] * lq[1:2], axis=1, keepdims=True))
           - jnp.exp(jnp.sum(lq[2:3] * lq[3:4], axis=1, keepdims=True)) + lam_init)

    def scores(t, slot):
        k = k_ref[pl.ds(pl.multiple_of(kj_ref[t] * tq, tq), tq), :]
        s_sc[slot] = lax.dot_general(jnp.concatenate([k, k_bias], axis=1), qs_sc[qi_ref[t]],
                                     (((1,), (1,)), ((), ())), preferred_element_type=F32)

    def value_product(j, ps):
        vt_aug = jnp.concatenate([vt_ref[j], ones_row], axis=0)
        return jnp.dot(vt_aug, jnp.concatenate(ps, axis=1), preferred_element_type=F32)

    def first_update(i, slot):
        ps = []
        for c in range(n_ct):
            cs = slice(c * LANES, (c + 1) * LANES)
            s = jnp.where(kpos <= lane + (c * LANES) % tq, s_sc[slot, :, cs], NEG)
            m_new = jnp.max(s, axis=0, keepdims=True)
            ps.append(jnp.exp2(s - m_new).astype(BF16))
            m_sc[i, :, cs] = m_new
        acc_sc[i] = value_product(i, ps)

    def update(i, j, slot):
        off = ((j - i) * tq).astype(F32) * slope2
        ps, alphas = [], []
        for c in range(n_ct):
            cs = slice(c * LANES, (c + 1) * LANES)
            s = s_sc[slot, :, cs]
            m_prev = m_sc[i, :, cs]
            m_new = jnp.maximum(m_prev, jnp.max(s, axis=0, keepdims=True) + off)
            alphas.append(jnp.exp2(m_prev - m_new))
            ps.append(jnp.exp2(s - (m_new - off)).astype(BF16))
            m_sc[i, :, cs] = m_new
        acc_sc[i] = jnp.concatenate(alphas, axis=1) * acc_sc[i] + value_product(j, ps)

    def finalize(i, carry):
        rows = pl.ds(pl.multiple_of(i * tq, tq), tq)
        ot = acc_sc[i, :dv, :] / acc_sc[i, dv:dv + 1, :]
        od = (ot[:, :tq] - lam * ot[:, tq:]).T
        ms = jnp.mean(od * od, axis=-1, keepdims=True)
        y = od * lax.rsqrt(ms + EPS) * na_ref[...] * (1.0 - lam_init)
        za = za_ref[rows, :].astype(F32)
        o_ref[rows, :] = (y * (za * _sigmoid(za))).astype(BF16)
        return carry

    scores(0, 0)

    def diag_body(tt, carry):
        t = 2 * tt
        scores(t + 1, 1)
        first_update(qi_ref[t], 0)
        scores(t + 2, 0)
        first_update(qi_ref[t + 1], 1)
        return carry

    lax.fori_loop(0, nq // 2, diag_body, 0)

    def off_body(tt, carry):
        t = nq + 2 * tt
        scores(t + 1, 1)
        update(qi_ref[t], kj_ref[t], 0)
        scores(t + 2, 0)
        update(qi_ref[t + 1], kj_ref[t + 1], 1)
        return carry

    lax.fori_loop(0, (n_steps - nq) // 2, off_body, 0)
    lax.fori_loop(0, nq, finalize, 0)


def _attention(p3, vt4, slopes, lq, norm_a, *, lam_init, tq=512):
    b, s, _ = p3.shape
    h = N_HEADS_A
    nq = s // tq
    nt = vt4.shape[1]
    assert vt4.shape[3] == tq and nt == nq and nq % 2 == 0 and (nq * (nq - 1) // 2) % 2 == 0
    pairs = ([(i, i) for i in range(nq)] + [(i, j) for i in range(nq) for j in range(i)] + [(0, 0)])
    qi = jnp.asarray([p[0] for p in pairs], jnp.int32)
    kj = jnp.asarray([p[1] for p in pairs], jnp.int32)
    res = lambda base: (lambda bi, hi: (bi, 0, base + hi))
    smem = pl.BlockSpec(memory_space=pltpu.SMEM)
    return pl.pallas_call(
        functools.partial(_attn_kernel, tq=tq, nq=nq, lam_init=lam_init),
        grid=(b, h),
        in_specs=[
            smem, smem, smem,
            pl.BlockSpec((4, HEAD_DIM_A), lambda bi, hi: (0, 0)),
            pl.BlockSpec((None, s, LANES), res(16)),
            pl.BlockSpec((None, s, LANES), res(20)),
            pl.BlockSpec((None, nt, LANES, tq), lambda bi, hi: (bi, 0, hi, 0)),
            pl.BlockSpec((None, s, LANES), res(24)),
            pl.BlockSpec((1, LANES), lambda bi, hi: (0, hi)),
        ],
        out_specs=pl.BlockSpec((None, s, LANES), lambda bi, hi: (bi, 0, hi)),
        out_shape=jax.ShapeDtypeStruct((b, s, h * LANES), BF16),
        scratch_shapes=[
            pltpu.VMEM((nq, 2 * tq, 2 * LANES), BF16),
            pltpu.VMEM((2, tq, 2 * tq), F32),
            pltpu.VMEM((nq, 1, 2 * tq), F32),
            pltpu.VMEM((nq, LANES + 2 * SUBLANES, 2 * tq), F32),
        ],
        compiler_params=pltpu.CompilerParams(
            dimension_semantics=("arbitrary", "arbitrary"), vmem_limit_bytes=VMEM_LIMIT),
        name="diffattn",
    )(slopes, qi, kj, lq, p3, p3, vt4, p3, norm_a)


def _shift_rows(x, tail, sh):
    r = pltpu.roll(x, sh, 0)
    rt = pltpu.roll(tail, sh, 0)
    row = lax.broadcasted_iota(jnp.int32, tail.shape, 0)
    first = jnp.where(row < sh, rt, r[:SUBLANES])
    return jnp.concatenate([first, r[SUBLANES:]], axis=0)


def _conv_silu(x_ref, tail_sc, w_ref):
    x = x_ref[...].astype(F32)
    tail = tail_sc[...]
    w = w_ref[...]
    y = x * w[CONV_W - 1:CONV_W]
    for tap in range(CONV_W - 1):
        y = y + _shift_rows(x, tail, CONV_W - 1 - tap) * w[tap:tap + 1]
    tail_sc[...] = x[x.shape[0] - SUBLANES:]
    return y * _sigmoid(y)


def _mlstm_kernel(bias_ref, cq_ref, ck_ref, nb_ref, q_ref, k_ref, v_ref, o_ref, z_ref, g_ref,
                  y_ref, c_sc, n_sc, m_sc, tq_sc, tk_sc, wt_sc, *, chunk):
    L = chunk
    nh, dh = N_HEADS_B, HEAD_DIM_B

    @pl.when(pl.program_id(1) == 0)
    def _():
        c_sc[...] = jnp.zeros(c_sc.shape, F32)
        n_sc[...] = jnp.zeros(n_sc.shape, F32)
        m_sc[...] = jnp.zeros(m_sc.shape, F32)
        tq_sc[...] = jnp.zeros(tq_sc.shape, F32)
        tk_sc[...] = jnp.zeros(tk_sc.shape, F32)

    q_all = _conv_silu(q_ref, tq_sc, cq_ref)
    k_all = _conv_silu(k_ref, tk_sc, ck_ref) * (dh ** -0.5)

    g = g_ref[...] + bias_ref[...]
    lf = jnp.minimum(g, 0.0) - jnp.log1p(jnp.exp(-jnp.abs(g)))
    rr = lax.broadcasted_iota(jnp.int32, (L, L), 0)
    cc = lax.broadcasted_iota(jnp.int32, (L, L), 1)
    causal = cc <= rr
    tri = jnp.where(causal, 1.0, 0.0).astype(BF16)
    cs = sum(jnp.dot(tri, part, preferred_element_type=F32) for part in _split3(lf))
    lane = lax.broadcasted_iota(jnp.int32, (L, LANES), 1)
    w = jnp.where(lane < nh, g, cs)
    wt_sc[...] = w.T
    w_last = w[L - 1:L, :]
    lane1 = lax.broadcasted_iota(jnp.int32, (1, LANES), 1)

    def pick(x, ln, idx):
        return jnp.sum(jnp.where(ln == idx, x, 0.0), axis=1, keepdims=True)

    for h in range(nh):
        hs = slice(h * dh, (h + 1) * dh)
        b_col = pick(w, lane, nh + h)
        i_col = pick(w, lane, h)
        b_row = wt_sc[nh + h:nh + h + 1, :]
        i_row = wt_sc[h:h + 1, :]
        b_last = pick(w_last, lane1, nh + h)
        m_prev = jnp.max(m_sc[h:h + 1, :], axis=1, keepdims=True)

        log_d = jnp.where(causal, b_col + (i_row - b_row), NEG)
        inter = b_col + m_prev
        m_row = jnp.maximum(inter, jnp.max(log_d, axis=1, keepdims=True))
        d_m = jnp.exp(log_d - m_row)
        w_inter = jnp.exp(inter - m_row)

        q = q_all[:, hs]
        k = k_all[:, hs]
        qb = q.astype(BF16)
        v = v_ref[:, hs]
        s_m = lax.dot_general(qb, k.astype(BF16), (((1,), (1,)), ((), ())),
                              preferred_element_type=F32) * d_m
        c_old = c_sc[h]
        n_old = n_sc[h]
        num = (jnp.dot(s_m.astype(BF16), v, preferred_element_type=F32)
               + w_inter * jnp.dot(qb, c_old.astype(BF16), preferred_element_type=F32))
        den = (jnp.sum(s_m, axis=1, keepdims=True)
               + w_inter * jnp.sum(q * n_old, axis=1, keepdims=True))
        hval = num / jnp.maximum(jnp.abs(den), jnp.exp(-m_row))

        a_col = b_last - b_col + i_col
        m_new = jnp.maximum(b_last + m_prev, jnp.max(a_col, axis=0, keepdims=True))
        decay = jnp.exp(b_last + m_prev - m_new)
        kw = k * jnp.exp(a_col - m_new)
        c_sc[h] = decay * c_old + jnp.dot(kw.T.astype(BF16), v, preferred_element_type=F32)
        n_sc[h] = decay * n_old + jnp.sum(kw, axis=0, keepdims=True)
        m_sc[h:h + 1, :] = jnp.broadcast_to(m_new, (1, LANES))

        hb = _sigmoid(o_ref[:, hs].astype(F32)) * hval
        ms = jnp.mean(hb * hb, axis=-1, keepdims=True)
        yb = hb * lax.rsqrt(ms + EPS) * nb_ref[:, hs]
        z = z_ref[:, hs].astype(F32)
        y_ref[:, hs] = (yb * (z * _sigmoid(z))).astype(BF16)


def _mlstm(p3, gates3, bias_row, conv_q, conv_k, norm_b, *, chunk=256):
    b, s, _ = p3.shape
    wb = N_HEADS_B * HEAD_DIM_B
    col = lambda idx: (lambda bi, ci: (bi, ci, idx))
    const = lambda bi, ci: (0, 0)
    return pl.pallas_call(
        functools.partial(_mlstm_kernel, chunk=chunk),
        grid=(b, s // chunk),
        in_specs=[
            pl.BlockSpec((1, LANES), const),
            pl.BlockSpec((CONV_W, wb), const),
            pl.BlockSpec((CONV_W, wb), const),
            pl.BlockSpec((1, wb), const),
            pl.BlockSpec((None, chunk, wb), col(7)),
            pl.BlockSpec((None, chunk, wb), col(8)),
            pl.BlockSpec((None, chunk, wb), col(9)),
            pl.BlockSpec((None, chunk, wb), col(10)),
            pl.BlockSpec((None, chunk, wb), col(11)),
            pl.BlockSpec((None, chunk, LANES), lambda bi, ci: (bi, ci, 0)),
        ],
        out_specs=pl.BlockSpec((None, chunk, wb), lambda bi, ci: (bi, ci, 0)),
        out_shape=jax.ShapeDtypeStruct((b, s, wb), BF16),
        scratch_shapes=[
            pltpu.VMEM((N_HEADS_B, HEAD_DIM_B, HEAD_DIM_B), F32),
            pltpu.VMEM((N_HEADS_B, 1, HEAD_DIM_B), F32),
            pltpu.VMEM((SUBLANES, LANES), F32),
            pltpu.VMEM((SUBLANES, wb), F32),
            pltpu.VMEM((SUBLANES, wb), F32),
            pltpu.VMEM((LANES, chunk), F32),
        ],
        compiler_params=pltpu.CompilerParams(
            dimension_semantics=("arbitrary", "arbitrary"), vmem_limit_bytes=VMEM_LIMIT),
        name="mlstm",
    )(bias_row, conv_q, conv_k, norm_b, p3, p3, p3, p3, p3, gates3)


def _merge_kernel(x_ref, ya_ref, yb_ref, ga_ref, gb_ref, wa_ref, wb_ref, wo_ref, g_ref, o_ref):
    a = jnp.dot(ya_ref[...], wa_ref[...], preferred_element_type=F32)
    b = jnp.dot(yb_ref[...], wb_ref[...], preferred_element_type=F32)
    merged = _sigmoid(ga_ref[...].astype(F32)) * a + _sigmoid(gb_ref[...].astype(F32)) * b
    out = jnp.dot(merged.astype(BF16), wo_ref[...], preferred_element_type=F32)
    ms = jnp.mean(out * out, axis=-1, keepdims=True)
    o_ref[...] = x_ref[...] + out * lax.rsqrt(ms + EPS) * g_ref[...]


def _merge(x2, ya2, yb2, p2, w_a, w_b, w_out, g, *, tm=512):
    m, d = x2.shape
    wa = ya2.shape[1]
    row = lambda i: (i, 0)
    const = lambda i: (0, 0)
    return pl.pallas_call(
        _merge_kernel,
        grid=(m // tm,),
        in_specs=[
            pl.BlockSpec((tm, d), row),
            pl.BlockSpec((tm, wa), row),
            pl.BlockSpec((tm, wa), row),
            pl.BlockSpec((tm, d), lambda i: (i, 0)),
            pl.BlockSpec((tm, d), lambda i: (i, 1)),
            pl.BlockSpec((wa, d), const),
            pl.BlockSpec((wa, d), const),
            pl.BlockSpec((d, d), const),
            pl.BlockSpec((1, d), const),
        ],
        out_specs=pl.BlockSpec((tm, d), row),
        out_shape=jax.ShapeDtypeStruct((m, d), F32),
        compiler_params=pltpu.CompilerParams(
            dimension_semantics=("arbitrary",), vmem_limit_bytes=VMEM_LIMIT),
        name="merge",
    )(x2, ya2, yb2, p2, p2, w_a, w_b, w_out, g)


def kernel(x, norm_pre, norm_post, w_in, b_if, conv_qk, lambda_qk, norm_a, norm_b, w_a, w_b, w_out):
    bsz, seq, d = x.shape
    depth = w_in.shape[0]
    wa = N_HEADS_A * 2 * HEAD_DIM_A
    wb = N_HEADS_B * HEAD_DIM_B
    n9 = 4 * wa + 5 * wb
    ng = 2 * N_HEADS_B
    assert w_in.shape[2] == n9 + ng + 2 * d and d == 1024 and wa == 512 and wb == 512
    slopes = jnp.asarray(2.0 ** (-8.0 * np.arange(1, N_HEADS_A + 1) / N_HEADS_A), F32)

    x2 = x.reshape(bsz * seq, d)
    for l in range(depth):
        w = w_in[l]
        w_main = jnp.concatenate(
            [w[:, n9 + ng:], w[:, :2 * wa], w[:, 3 * wa:n9]], axis=1).astype(BF16)
        w_vt = w[:, 2 * wa:3 * wa].T.astype(BF16)
        w_gate = jnp.pad(w[:, n9:n9 + ng], ((0, 0), (0, LANES - ng))).astype(BF16)
        bias_row = jnp.pad(b_if[l], (0, LANES - ng)).reshape(1, LANES)
        lam_init = 0.8 - 0.6 * math.exp(-0.3 * l)

        p2, gates2, vt4 = _inproj(x2, norm_pre[l].reshape(1, d), w_main, w_gate, w_vt, seq=seq)
        p3 = p2.reshape(bsz, seq, -1)
        ya = _attention(p3, vt4, slopes, lambda_qk[l], norm_a[l].reshape(1, wa), lam_init=lam_init)
        yb = _mlstm(p3, gates2.reshape(bsz, seq, LANES), bias_row,
                    conv_qk[l][:, :wb], conv_qk[l][:, wb:], norm_b[l].reshape(1, wb))
        x2 = _merge(x2, ya.reshape(bsz * seq, wa), yb.reshape(bsz * seq, wb), p2,
                    w_a[l].astype(BF16), w_b[l].astype(BF16), w_out[l].astype(BF16),
                    norm_post[l].reshape(1, d))
    return x2.reshape(bsz, seq, d)
```

```python
import functools
import math

import jax
import jax.numpy as jnp
import numpy as np
from jax import lax
from jax.experimental import pallas as pl
from jax.experimental.pallas import tpu as pltpu

F32 = jnp.float32
BF16 = jnp.bfloat16

N_HEADS_A = 4
HEAD_DIM_A = 64
N_HEADS_B = 4
HEAD_DIM_B = 128
CONV_W = 4
EPS = 1e-6

LANES = 128
SUBLANES = 8
NEG = -1e30
LOG2E = math.log2(math.e)
VMEM_LIMIT = 56 * 1024 * 1024


def _sigmoid(x):
    return jax.nn.sigmoid(x)


def _split3(x):
    hi = x.astype(BF16)
    r1 = x - hi.astype(F32)
    mid = r1.astype(BF16)
    lo = (r1 - mid.astype(F32)).astype(BF16)
    return hi, mid, lo


def _inproj_kernel(x_ref, g_ref, w9_ref, wgg_ref, wg_ref, wvt_ref, p_ref, gate_ref, vt_ref, *, tn, src_chunks):
    x = x_ref[...]
    ms = jnp.mean(x * x, axis=-1, keepdims=True)
    h = (x * lax.rsqrt(ms + EPS) * g_ref[...]).astype(BF16)
    n_gg = wgg_ref.shape[1] // tn
    for c in range(n_gg):
        cs = slice(c * tn, (c + 1) * tn)
        p_ref[:, cs] = jnp.dot(h, wgg_ref[:, cs], preferred_element_type=F32).astype(BF16)
    for c, src in enumerate(src_chunks):
        p_ref[:, (n_gg + c) * tn:(n_gg + c + 1) * tn] = jnp.dot(
            h, w9_ref[:, src * tn:(src + 1) * tn], preferred_element_type=F32).astype(BF16)
    gate_ref[...] = jnp.dot(h, wg_ref[...], preferred_element_type=F32)
    vt_ref[...] = lax.dot_general(wvt_ref[...], h, (((1,), (1,)), ((), ())),
                                  preferred_element_type=F32).astype(BF16)


def _inproj(x2, g, w_all, layer, n9, w_gg, w_gate, w_vt, *, seq, tm=512, tn=512):
    m, d = x2.shape
    src_chunks = tuple(c for c in range(n9 // tn) if c != 2)
    n_main = w_gg.shape[1] + len(src_chunks) * tn
    wv = w_vt.shape[0]
    nt = seq // tm
    once = dict(pipeline_mode=pl.Buffered(1))
    return pl.pallas_call(
        functools.partial(_inproj_kernel, tn=tn, src_chunks=src_chunks),
        grid=(m // tm,),
        in_specs=[
            pl.BlockSpec((tm, d), lambda i: (i, 0)),
            pl.BlockSpec((1, d), lambda i: (0, 0)),
            pl.BlockSpec((None, d, n9), lambda i: (layer, 0, 0), **once),
            pl.BlockSpec((d, w_gg.shape[1]), lambda i: (0, 0), **once),
            pl.BlockSpec((d, LANES), lambda i: (0, 0)),
            pl.BlockSpec((wv, d), lambda i: (0, 0)),
        ],
        out_specs=[
            pl.BlockSpec((tm, n_main), lambda i: (i, 0)),
            pl.BlockSpec((tm, LANES), lambda i: (i, 0)),
            pl.BlockSpec((None, None, wv, tm), lambda i: (i // nt, i % nt, 0, 0)),
        ],
        out_shape=[
            jax.ShapeDtypeStruct((m, n_main), BF16),
            jax.ShapeDtypeStruct((m, LANES), F32),
            jax.ShapeDtypeStruct((m // seq, nt, wv, tm), BF16),
        ],
        compiler_params=pltpu.CompilerParams(
            dimension_semantics=("arbitrary",), vmem_limit_bytes=VMEM_LIMIT),
        name="inproj",
    )(x2, g, w_all, w_gg, w_gate, w_vt)


def _attn_kernel(slopes_ref, qi_ref, kj_ref, lq_ref, q_ref, k_ref, vt_ref, za_ref, na_ref, o_ref,
                 qs_sc, s_sc, m_sc, acc_sc, *, tq, nq, lam_init):
    h = pl.program_id(1)
    d = HEAD_DIM_A
    dv = 2 * d
    slope2 = slopes_ref[h] * LOG2E
    rows2 = 2 * tq
    n_ct = rows2 // LANES
    n_steps = nq * (nq + 1) // 2
    off_unroll = 4
    assert (n_steps - nq) % off_unroll == 0

    lane = lax.broadcasted_iota(jnp.int32, (tq, LANES), 1)
    lane2 = lax.broadcasted_iota(jnp.int32, (rows2, LANES), 1)
    ones_cols = jnp.where(lane2 < 3, 1.0, 0.0).astype(BF16)
    for qb in range(nq):
        q = (q_ref[qb * tq:(qb + 1) * tq, :].astype(F32) * (d ** -0.5 * LOG2E)).astype(BF16)
        zero = jnp.zeros_like(q)
        qs = jnp.concatenate([jnp.where(lane < d, q, zero), jnp.where(lane >= d, q, zero)], axis=0)
        qs_sc[qb] = jnp.concatenate([qs, ones_cols], axis=1)
    kpos = lax.broadcasted_iota(jnp.int32, (tq, LANES), 0)
    b_hi, b_mid, b_lo = (part.astype(F32) for part in _split3(kpos.astype(F32) * slope2))
    k_bias = jnp.where(lane == 0, b_hi, jnp.where(lane == 1, b_mid, jnp.where(lane == 2, b_lo, 0.0)))
    k_bias = k_bias.astype(BF16)
    ones_row = jnp.where(lax.broadcasted_iota(jnp.int32, (2 * SUBLANES, tq), 0) == 0, 1.0, 0.0).astype(BF16)

    lq = lq_ref[...]
    lam = (jnp.exp(jnp.sum(lq[0:1] * lq[1:2], axis=1, keepdims=True))
           - jnp.exp(jnp.sum(lq[2:3] * lq[3:4], axis=1, keepdims=True)) + lam_init)

    def scores(t, slot):
        k = k_ref[pl.ds(pl.multiple_of(kj_ref[t] * tq, tq), tq), :]
        s_sc[slot] = lax.dot_general(jnp.concatenate([k, k_bias], axis=1), qs_sc[qi_ref[t]],
                                     (((1,), (1,)), ((), ())), preferred_element_type=F32)

    def value_product(j, ps):
        vt_aug = jnp.concatenate([vt_ref[j], ones_row], axis=0)
        return jnp.dot(vt_aug, jnp.concatenate(ps, axis=1), preferred_element_type=F32)

    def first_update(i, slot):
        ps = []
        for c in range(n_ct):
            cs = slice(c * LANES, (c + 1) * LANES)
            s = jnp.where(kpos <= lane + (c * LANES) % tq, s_sc[slot, :, cs], NEG)
            m_new = jnp.max(s, axis=0, keepdims=True)
            ps.append(jnp.exp2(s - m_new).astype(BF16))
            m_sc[i, :, cs] = m_new
        acc_sc[i] = value_product(i, ps)

    def update(i, j, slot):
        off = ((j - i) * tq).astype(F32) * slope2
        ps, alphas = [], []
        for c in range(n_ct):
            cs = slice(c * LANES, (c + 1) * LANES)
            s = s_sc[slot, :, cs]
            m_prev = m_sc[i, :, cs]
            m_new = jnp.maximum(m_prev, jnp.max(s, axis=0, keepdims=True) + off)
            alphas.append(jnp.exp2(m_prev - m_new))
            ps.append(jnp.exp2(s - (m_new - off)).astype(BF16))
            m_sc[i, :, cs] = m_new
        acc_sc[i] = jnp.concatenate(alphas, axis=1) * acc_sc[i] + value_product(j, ps)

    def finalize(i, carry):
        rows = pl.ds(pl.multiple_of(i * tq, tq), tq)
        ot = acc_sc[i, :dv, :] / acc_sc[i, dv:dv + 1, :]
        od = (ot[:, :tq] - lam * ot[:, tq:]).T
        ms = jnp.mean(od * od, axis=-1, keepdims=True)
        y = od * lax.rsqrt(ms + EPS) * na_ref[...] * (1.0 - lam_init)
        za = za_ref[rows, :].astype(F32)
        o_ref[rows, :] = (y * (za * _sigmoid(za))).astype(BF16)
        return carry

    scores(0, 0)

    def diag_body(tt, carry):
        t = 2 * tt
        scores(t + 1, 1)
        first_update(qi_ref[t], 0)
        scores(t + 2, 0)
        first_update(qi_ref[t + 1], 1)
        return carry

    lax.fori_loop(0, nq // 2, diag_body, 0)

    def off_body(tt, carry):
        for u in range(0, off_unroll, 2):
            t = nq + off_unroll * tt + u
            scores(t + 1, 1)
            update(qi_ref[t], kj_ref[t], 0)
            scores(t + 2, 0)
            update(qi_ref[t + 1], kj_ref[t + 1], 1)
        return carry

    lax.fori_loop(0, (n_steps - nq) // off_unroll, off_body, 0)
    lax.fori_loop(0, nq, finalize, 0)


def _attention(p3, vt4, slopes, lq, norm_a, *, lam_init, tq=512):
    b, s, _ = p3.shape
    h = N_HEADS_A
    nq = s // tq
    nt = vt4.shape[1]
    assert vt4.shape[3] == tq and nt == nq and nq % 2 == 0 and (nq * (nq - 1) // 2) % 2 == 0
    pairs = ([(i, i) for i in range(nq)] + [(i, j) for i in range(nq) for j in range(i)] + [(0, 0)])
    qi = jnp.asarray([p[0] for p in pairs], jnp.int32)
    kj = jnp.asarray([p[1] for p in pairs], jnp.int32)
    res = lambda base: (lambda bi, hi: (bi, 0, base + hi))
    smem = pl.BlockSpec(memory_space=pltpu.SMEM)
    return pl.pallas_call(
        functools.partial(_attn_kernel, tq=tq, nq=nq, lam_init=lam_init),
        grid=(b, h),
        in_specs=[
            smem, smem, smem,
            pl.BlockSpec((4, HEAD_DIM_A), lambda bi, hi: (0, 0)),
            pl.BlockSpec((None, s, LANES), res(16)),
            pl.BlockSpec((None, s, LANES), res(20)),
            pl.BlockSpec((None, nt, LANES, tq), lambda bi, hi: (bi, 0, hi, 0)),
            pl.BlockSpec((None, s, LANES), res(24)),
            pl.BlockSpec((1, LANES), lambda bi, hi: (0, hi)),
        ],
        out_specs=pl.BlockSpec((None, s, LANES), lambda bi, hi: (bi, 0, hi)),
        out_shape=jax.ShapeDtypeStruct((b, s, h * LANES), BF16),
        scratch_shapes=[
            pltpu.VMEM((nq, 2 * tq, 2 * LANES), BF16),
            pltpu.VMEM((2, tq, 2 * tq), F32),
            pltpu.VMEM((nq, 1, 2 * tq), F32),
            pltpu.VMEM((nq, LANES + 2 * SUBLANES, 2 * tq), F32),
        ],
        compiler_params=pltpu.CompilerParams(
            dimension_semantics=("arbitrary", "arbitrary"), vmem_limit_bytes=VMEM_LIMIT),
        name="diffattn",
    )(slopes, qi, kj, lq, p3, p3, vt4, p3, norm_a)


def _shift_rows(x, tail, sh):
    r = pltpu.roll(x, sh, 0)
    rt = pltpu.roll(tail, sh, 0)
    row = lax.broadcasted_iota(jnp.int32, tail.shape, 0)
    first = jnp.where(row < sh, rt, r[:SUBLANES])
    return jnp.concatenate([first, r[SUBLANES:]], axis=0)


def _conv_silu(x_ref, tail_sc, w_ref):
    x = x_ref[...].astype(F32)
    tail = tail_sc[...]
    w = w_ref[...]
    y = x * w[CONV_W - 1:CONV_W]
    for tap in range(CONV_W - 1):
        y = y + _shift_rows(x, tail, CONV_W - 1 - tap) * w[tap:tap + 1]
    tail_sc[...] = x[x.shape[0] - SUBLANES:]
    return y * _sigmoid(y)


def _mlstm_kernel(bias_ref, cq_ref, ck_ref, nb_ref, q_ref, k_ref, v_ref, o_ref, z_ref, g_ref,
                  y_ref, c_sc, n_sc, m_sc, tq_sc, tk_sc, wt_sc, *, chunk):
    L = chunk
    nh, dh = N_HEADS_B, HEAD_DIM_B

    @pl.when(pl.program_id(1) == 0)
    def _():
        c_sc[...] = jnp.zeros(c_sc.shape, F32)
        n_sc[...] = jnp.zeros(n_sc.shape, F32)
        m_sc[...] = jnp.zeros(m_sc.shape, F32)
        tq_sc[...] = jnp.zeros(tq_sc.shape, F32)
        tk_sc[...] = jnp.zeros(tk_sc.shape, F32)

    q_all = _conv_silu(q_ref, tq_sc, cq_ref)
    k_all = _conv_silu(k_ref, tk_sc, ck_ref) * (dh ** -0.5)

    g = g_ref[...] + bias_ref[...]
    lf = jnp.minimum(g, 0.0) - jnp.log1p(jnp.exp(-jnp.abs(g)))
    rr = lax.broadcasted_iota(jnp.int32, (L, L), 0)
    cc = lax.broadcasted_iota(jnp.int32, (L, L), 1)
    causal = cc <= rr
    tri = jnp.where(causal, 1.0, 0.0).astype(BF16)
    cs = sum(jnp.dot(tri, part, preferred_element_type=F32) for part in _split3(lf))
    lane = lax.broadcasted_iota(jnp.int32, (L, LANES), 1)
    w = jnp.where(lane < nh, g, cs)
    wt_sc[...] = w.T
    w_last = w[L - 1:L, :]
    lane1 = lax.broadcasted_iota(jnp.int32, (1, LANES), 1)

    def pick(x, ln, idx):
        return jnp.sum(jnp.where(ln == idx, x, 0.0), axis=1, keepdims=True)

    for h in range(nh):
        hs = slice(h * dh, (h + 1) * dh)
        b_col = pick(w, lane, nh + h)
        i_col = pick(w, lane, h)
        b_row = wt_sc[nh + h:nh + h + 1, :]
        i_row = wt_sc[h:h + 1, :]
        b_last = pick(w_last, lane1, nh + h)
        m_prev = jnp.max(m_sc[h:h + 1, :], axis=1, keepdims=True)

        log_d = jnp.where(causal, b_col + (i_row - b_row), NEG)
        inter = b_col + m_prev
        m_row = jnp.maximum(inter, jnp.max(log_d, axis=1, keepdims=True))
        d_m = jnp.exp(log_d - m_row)
        w_inter = jnp.exp(inter - m_row)

        q = q_all[:, hs]
        k = k_all[:, hs]
        qb = q.astype(BF16)
        v = v_ref[:, hs]
        s_m = lax.dot_general(qb, k.astype(BF16), (((1,), (1,)), ((), ())),
                              preferred_element_type=F32) * d_m
        c_old = c_sc[h]
        n_old = n_sc[h]
        num = (jnp.dot(s_m.astype(BF16), v, preferred_element_type=F32)
               + w_inter * jnp.dot(qb, c_old.astype(BF16), preferred_element_type=F32))
        den = (jnp.sum(s_m, axis=1, keepdims=True)
               + w_inter * jnp.sum(q * n_old, axis=1, keepdims=True))
        hval = num / jnp.maximum(jnp.abs(den), jnp.exp(-m_row))

        a_col = b_last - b_col + i_col
        m_new = jnp.maximum(b_last + m_prev, jnp.max(a_col, axis=0, keepdims=True))
        decay = jnp.exp(b_last + m_prev - m_new)
        kw = k * jnp.exp(a_col - m_new)
        c_sc[h] = decay * c_old + jnp.dot(kw.T.astype(BF16), v, preferred_element_type=F32)
        n_sc[h] = decay * n_old + jnp.sum(kw, axis=0, keepdims=True)
        m_sc[h:h + 1, :] = jnp.broadcast_to(m_new, (1, LANES))

        hb = _sigmoid(o_ref[:, hs].astype(F32)) * hval
        ms = jnp.mean(hb * hb, axis=-1, keepdims=True)
        yb = hb * lax.rsqrt(ms + EPS) * nb_ref[:, hs]
        z = z_ref[:, hs].astype(F32)
        y_ref[:, hs] = (yb * (z * _sigmoid(z))).astype(BF16)


def _mlstm(p3, gates3, bias_row, conv_q, conv_k, norm_b, *, chunk=256):
    b, s, _ = p3.shape
    wb = N_HEADS_B * HEAD_DIM_B
    col = lambda idx: (lambda bi, ci: (bi, ci, idx))
    const = lambda bi, ci: (0, 0)
    return pl.pallas_call(
        functools.partial(_mlstm_kernel, chunk=chunk),
        grid=(b, s // chunk),
        in_specs=[
            pl.BlockSpec((1, LANES), const),
            pl.BlockSpec((CONV_W, wb), const),
            pl.BlockSpec((CONV_W, wb), const),
            pl.BlockSpec((1, wb), const),
            pl.BlockSpec((None, chunk, wb), col(7)),
            pl.BlockSpec((None, chunk, wb), col(8)),
            pl.BlockSpec((None, chunk, wb), col(9)),
            pl.BlockSpec((None, chunk, wb), col(10)),
            pl.BlockSpec((None, chunk, wb), col(11)),
            pl.BlockSpec((None, chunk, LANES), lambda bi, ci: (bi, ci, 0)),
        ],
        out_specs=pl.BlockSpec((None, chunk, wb), lambda bi, ci: (bi, ci, 0)),
        out_shape=jax.ShapeDtypeStruct((b, s, wb), BF16),
        scratch_shapes=[
            pltpu.VMEM((N_HEADS_B, HEAD_DIM_B, HEAD_DIM_B), F32),
            pltpu.VMEM((N_HEADS_B, 1, HEAD_DIM_B), F32),
            pltpu.VMEM((SUBLANES, LANES), F32),
            pltpu.VMEM((SUBLANES, wb), F32),
            pltpu.VMEM((SUBLANES, wb), F32),
            pltpu.VMEM((LANES, chunk), F32),
        ],
        compiler_params=pltpu.CompilerParams(
            dimension_semantics=("arbitrary", "arbitrary"), vmem_limit_bytes=VMEM_LIMIT),
        name="mlstm",
    )(bias_row, conv_q, conv_k, norm_b, p3, p3, p3, p3, p3, gates3)


def _merge_kernel(x_ref, ya_ref, yb_ref, ga_ref, gb_ref, wa_ref, wb_ref, wo_ref, g_ref, o_ref):
    a = jnp.dot(ya_ref[...], wa_ref[...], preferred_element_type=F32)
    b = jnp.dot(yb_ref[...], wb_ref[...], preferred_element_type=F32)
    merged = _sigmoid(ga_ref[...].astype(F32)) * a + _sigmoid(gb_ref[...].astype(F32)) * b
    out = jnp.dot(merged.astype(BF16), wo_ref[...], preferred_element_type=F32)
    ms = jnp.mean(out * out, axis=-1, keepdims=True)
    o_ref[...] = x_ref[...] + out * lax.rsqrt(ms + EPS) * g_ref[...]


def _merge(x2, ya2, yb2, p2, w_a, w_b, w_out, layer, g, *, tm=512):
    m, d = x2.shape
    wa = ya2.shape[1]
    row = lambda i: (i, 0)
    const = lambda i: (0, 0)
    return pl.pallas_call(
        _merge_kernel,
        grid=(m // tm,),
        in_specs=[
            pl.BlockSpec((tm, d), row),
            pl.BlockSpec((tm, wa), row),
            pl.BlockSpec((tm, wa), row),
            pl.BlockSpec((tm, d), lambda i: (i, 0)),
            pl.BlockSpec((tm, d), lambda i: (i, 1)),
            pl.BlockSpec((None, wa, d), lambda i: (layer, 0, 0)),
            pl.BlockSpec((None, wa, d), lambda i: (layer, 0, 0)),
            pl.BlockSpec((None, d, d), lambda i: (layer, 0, 0)),
            pl.BlockSpec((1, d), const),
        ],
        out_specs=pl.BlockSpec((tm, d), row),
        out_shape=jax.ShapeDtypeStruct((m, d), F32),
        compiler_params=pltpu.CompilerParams(
            dimension_semantics=("arbitrary",), vmem_limit_bytes=VMEM_LIMIT),
        name="merge",
    )(x2, ya2, yb2, p2, p2, w_a, w_b, w_out, g)


def kernel(x, norm_pre, norm_post, w_in, b_if, conv_qk, lambda_qk, norm_a, norm_b, w_a, w_b, w_out):
    bsz, seq, d = x.shape
    depth = w_in.shape[0]
    wa = N_HEADS_A * 2 * HEAD_DIM_A
    wb = N_HEADS_B * HEAD_DIM_B
    n9 = 4 * wa + 5 * wb
    ng = 2 * N_HEADS_B
    assert w_in.shape[2] == n9 + ng + 2 * d and d == 1024 and wa == 512 and wb == 512
    slopes = jnp.asarray(2.0 ** (-8.0 * np.arange(1, N_HEADS_A + 1) / N_HEADS_A), F32)

    w_in_b = w_in.astype(BF16)
    w_a_b, w_b_b, w_out_b = w_a.astype(BF16), w_b.astype(BF16), w_out.astype(BF16)

    x2 = x.reshape(bsz * seq, d)
    for l in range(depth):
        w_gg = w_in_b[l, :, n9 + ng:]
        w_vt = w_in_b[l, :, 2 * wa:3 * wa].T
        w_gate = jnp.pad(w_in_b[l, :, n9:n9 + ng], ((0, 0), (0, LANES - ng)))
        bias_row = jnp.pad(b_if[l], (0, LANES - ng)).reshape(1, LANES)
        lam_init = 0.8 - 0.6 * math.exp(-0.3 * l)

        p2, gates2, vt4 = _inproj(x2, norm_pre[l].reshape(1, d), w_in_b, l, n9, w_gg, w_gate, w_vt, seq=seq)
        p3 = p2.reshape(bsz, seq, -1)
        ya = _attention(p3, vt4, slopes, lambda_qk[l], norm_a[l].reshape(1, wa), lam_init=lam_init)
        yb = _mlstm(p3, gates2.reshape(bsz, seq, LANES), bias_row,
                    conv_qk[l][:, :wb], conv_qk[l][:, wb:], norm_b[l].reshape(1, wb))
        x2 = _merge(x2, ya.reshape(bsz * seq, wa), yb.reshape(bsz * seq, wb), p2,
                    w_a_b, w_b_b, w_out_b, l, norm_post[l].reshape(1, d))
    return x2.reshape(bsz, seq, d)
```

```python
import functools
import math

import jax
import jax.numpy as jnp
import numpy as np
from jax import lax
from jax.experimental import pallas as pl
from jax.experimental.pallas import tpu as pltpu

F32 = jnp.float32
BF16 = jnp.bfloat16

N_HEADS_A = 4
HEAD_DIM_A = 64
N_HEADS_B = 4
HEAD_DIM_B = 128
CONV_W = 4
EPS = 1e-6

LANES = 128
SUBLANES = 8
NEG = -1e30
LOG2E = math.log2(math.e)
VMEM_LIMIT = 56 * 1024 * 1024


def _sigmoid(x):
    return jax.nn.sigmoid(x)


def _split3(x):
    hi = x.astype(BF16)
    r1 = x - hi.astype(F32)
    mid = r1.astype(BF16)
    lo = (r1 - mid.astype(F32)).astype(BF16)
    return hi, mid, lo


def _shift_rows(x, tail, sh):
    r = pltpu.roll(x, sh, 0)
    rt = pltpu.roll(tail, sh, 0)
    row = lax.broadcasted_iota(jnp.int32, tail.shape, 0)
    first = jnp.where(row < sh, rt, r[:SUBLANES])
    return jnp.concatenate([first, r[SUBLANES:]], axis=0)


def _conv_silu(xb, tail_sc, w_ref):
    x = xb.astype(F32)
    tail = tail_sc[...]
    w = w_ref[...]
    y = x * w[CONV_W - 1:CONV_W]
    for tap in range(CONV_W - 1):
        y = y + _shift_rows(x, tail, CONV_W - 1 - tap) * w[tap:tap + 1]
    tail_sc[...] = x[x.shape[0] - SUBLANES:]
    return y * _sigmoid(y)


def _mlstm_gates(r0, L, pm_sc, g_sc, bias_ref, cq_ref, ck_ref, tq_sc, tk_sc, wt_sc):
    nh, dh = N_HEADS_B, HEAD_DIM_B
    wb = nh * dh
    rows = slice(r0, r0 + L)
    q_all = _conv_silu(pm_sc[rows, 0:wb], tq_sc, cq_ref)
    k_all = _conv_silu(pm_sc[rows, wb:2 * wb], tk_sc, ck_ref) * (dh ** -0.5)

    g = g_sc[rows, :] + bias_ref[...]
    lf = jnp.minimum(g, 0.0) - jnp.log1p(jnp.exp(-jnp.abs(g)))
    rr = lax.broadcasted_iota(jnp.int32, (L, L), 0)
    cc = lax.broadcasted_iota(jnp.int32, (L, L), 1)
    causal = cc <= rr
    tri = jnp.where(causal, 1.0, 0.0).astype(BF16)
    cs = sum(jnp.dot(tri, part, preferred_element_type=F32) for part in _split3(lf))
    lane = lax.broadcasted_iota(jnp.int32, (L, LANES), 1)
    w = jnp.where(lane < nh, g, cs)
    wt_sc[...] = w.T
    return q_all, k_all, w, causal, lane


def _mlstm_head(h, r0, L, gates, pm_sc, nb_ref, y_ref, c_sc, n_sc, m_sc, wt_sc):
    nh, dh = N_HEADS_B, HEAD_DIM_B
    wb = nh * dh
    rows = slice(r0, r0 + L)
    hs = slice(h * dh, (h + 1) * dh)
    q_all, k_all, w, causal, lane = gates
    w_last = w[L - 1:L, :]
    lane1 = lax.broadcasted_iota(jnp.int32, (1, LANES), 1)

    def pick(x, ln, idx):
        return jnp.sum(jnp.where(ln == idx, x, 0.0), axis=1, keepdims=True)

    b_col = pick(w, lane, nh + h)
    i_col = pick(w, lane, h)
    b_row = wt_sc[nh + h:nh + h + 1, :]
    i_row = wt_sc[h:h + 1, :]
    b_last = pick(w_last, lane1, nh + h)
    m_prev = jnp.max(m_sc[h:h + 1, :], axis=1, keepdims=True)

    log_d = jnp.where(causal, b_col + (i_row - b_row), NEG)
    inter = b_col + m_prev
    m_row = jnp.maximum(inter, jnp.max(log_d, axis=1, keepdims=True))
    d_m = jnp.exp(log_d - m_row)
    w_inter = jnp.exp(inter - m_row)

    q = q_all[:, hs]
    k = k_all[:, hs]
    qb = q.astype(BF16)
    v = pm_sc[rows, 2 * wb + h * dh:2 * wb + (h + 1) * dh]
    s_m = lax.dot_general(qb, k.astype(BF16), (((1,), (1,)), ((), ())),
                          preferred_element_type=F32) * d_m
    c_old = c_sc[h]
    n_old = n_sc[h]
    num = (jnp.dot(s_m.astype(BF16), v, preferred_element_type=F32)
           + w_inter * jnp.dot(qb, c_old.astype(BF16), preferred_element_type=F32))
    den = (jnp.sum(s_m, axis=1, keepdims=True)
           + w_inter * jnp.sum(q * n_old, axis=1, keepdims=True))
    hval = num / jnp.maximum(jnp.abs(den), jnp.exp(-m_row))

    a_col = b_last - b_col + i_col
    m_new = jnp.maximum(b_last + m_prev, jnp.max(a_col, axis=0, keepdims=True))
    decay = jnp.exp(b_last + m_prev - m_new)
    kw = k * jnp.exp(a_col - m_new)
    c_sc[h] = decay * c_old + jnp.dot(kw.T.astype(BF16), v, preferred_element_type=F32)
    n_sc[h] = decay * n_old + jnp.sum(kw, axis=0, keepdims=True)
    m_sc[h:h + 1, :] = jnp.broadcast_to(m_new, (1, LANES))

    o = pm_sc[rows, 3 * wb + h * dh:3 * wb + (h + 1) * dh].astype(F32)
    hb = _sigmoid(o) * hval
    ms = jnp.mean(hb * hb, axis=-1, keepdims=True)
    yb = hb * lax.rsqrt(ms + EPS) * nb_ref[:, hs]
    z = pm_sc[rows, 4 * wb + h * dh:4 * wb + (h + 1) * dh].astype(F32)
    y_ref[rows, hs] = (yb * (z * _sigmoid(z))).astype(BF16)


def _proj_kernel(x_ref, g_ref, w9_ref, wgg_ref, wg_ref, wvt_ref, bias_ref, cq_ref, ck_ref, nb_ref,
                 p_ref, vt_ref, y_ref, pm_sc, g_sc, c_sc, n_sc, m_sc, tq_sc, tk_sc, wt_sc,
                 *, tn, chunk, steps_per_seq, attn_chunks, mlstm_chunks):
    @pl.when(pl.program_id(0) % steps_per_seq == 0)
    def _():
        c_sc[...] = jnp.zeros(c_sc.shape, F32)
        n_sc[...] = jnp.zeros(n_sc.shape, F32)
        m_sc[...] = jnp.zeros(m_sc.shape, F32)
        tq_sc[...] = jnp.zeros(tq_sc.shape, F32)
        tk_sc[...] = jnp.zeros(tk_sc.shape, F32)

    x = x_ref[...]
    ms = jnp.mean(x * x, axis=-1, keepdims=True)
    h = (x * lax.rsqrt(ms + EPS) * g_ref[...]).astype(BF16)
    n_chunks = x.shape[0] // chunk
    n_gg = wgg_ref.shape[1] // tn

    def project(w_ref, src):
        return jnp.dot(h, w_ref[:, src * tn:(src + 1) * tn], preferred_element_type=F32).astype(BF16)

    def to_pm(c):
        pm_sc[:, c * tn:(c + 1) * tn] = project(w9_ref, mlstm_chunks[c])

    def to_p_gg(c):
        p_ref[:, c * tn:(c + 1) * tn] = project(wgg_ref, c)

    def to_p_attn(c):
        p_ref[:, (n_gg + c) * tn:(n_gg + c + 1) * tn] = project(w9_ref, attn_chunks[c])

    def to_vt():
        vt_ref[...] = lax.dot_general(wvt_ref[...], h, (((1,), (1,)), ((), ())),
                                      preferred_element_type=F32).astype(BF16)

    fillers = ([functools.partial(to_p_gg, c) for c in range(n_gg)]
               + [functools.partial(to_p_attn, c) for c in range(len(attn_chunks))] + [to_vt])
    to_pm(0)
    to_pm(1)
    g_sc[...] = jnp.dot(h, wg_ref[...], preferred_element_type=F32)
    gates = []
    for ci in range(n_chunks):
        gates.append(_mlstm_gates(ci * chunk, chunk, pm_sc, g_sc, bias_ref, cq_ref, ck_ref,
                                  tq_sc, tk_sc, wt_sc.at[ci]))
        if 2 + ci < len(mlstm_chunks):
            to_pm(2 + ci)
    for c in range(2 + n_chunks, len(mlstm_chunks)):
        to_pm(c)
    for ci in range(n_chunks):
        for hd in range(N_HEADS_B):
            if fillers:
                fillers.pop(0)()
            _mlstm_head(hd, ci * chunk, chunk, gates[ci], pm_sc, nb_ref, y_ref,
                        c_sc, n_sc, m_sc, wt_sc.at[ci])
    for filler in fillers:
        filler()


def _proj_mlstm(x2, g, w9, w_gg, w_gate, w_vt, layer, bias_row, conv_q, conv_k, norm_b,
                *, seq, tm=512, tn=512, chunk=256):
    m, d = x2.shape
    n9 = w9.shape[2]
    ngg = w_gg.shape[2]
    wv = w_vt.shape[1]
    wb = N_HEADS_B * HEAD_DIM_B
    attn_chunks, mlstm_chunks = (0, 1, 3), (4, 5, 6, 7, 8)
    assert n9 == 9 * tn and wb == tn and wv == tn
    n_main = ngg + len(attn_chunks) * tn
    nt = seq // tm
    layer_blk = lambda i: (layer, 0, 0)
    const = lambda i: (0, 0)
    once = dict(pipeline_mode=pl.Buffered(1))
    return pl.pallas_call(
        functools.partial(_proj_kernel, tn=tn, chunk=chunk, steps_per_seq=nt,
                          attn_chunks=attn_chunks, mlstm_chunks=mlstm_chunks),
        grid=(m // tm,),
        in_specs=[
            pl.BlockSpec((tm, d), lambda i: (i, 0)),
            pl.BlockSpec((1, d), const),
            pl.BlockSpec((None, d, n9), layer_blk, **once),
            pl.BlockSpec((None, d, ngg), layer_blk, **once),
            pl.BlockSpec((None, d, LANES), layer_blk),
            pl.BlockSpec((None, wv, d), layer_blk),
            pl.BlockSpec((1, LANES), const),
            pl.BlockSpec((CONV_W, wb), const),
            pl.BlockSpec((CONV_W, wb), const),
            pl.BlockSpec((1, wb), const),
        ],
        out_specs=[
            pl.BlockSpec((tm, n_main), lambda i: (i, 0)),
            pl.BlockSpec((None, None, wv, tm), lambda i: (i // nt, i % nt, 0, 0)),
            pl.BlockSpec((tm, wb), lambda i: (i, 0)),
        ],
        out_shape=[
            jax.ShapeDtypeStruct((m, n_main), BF16),
            jax.ShapeDtypeStruct((m // seq, nt, wv, tm), BF16),
            jax.ShapeDtypeStruct((m, wb), BF16),
        ],
        scratch_shapes=[
            pltpu.VMEM((tm, len(mlstm_chunks) * tn), BF16),
            pltpu.VMEM((tm, LANES), F32),
            pltpu.VMEM((N_HEADS_B, HEAD_DIM_B, HEAD_DIM_B), F32),
            pltpu.VMEM((N_HEADS_B, 1, HEAD_DIM_B), F32),
            pltpu.VMEM((SUBLANES, LANES), F32),
            pltpu.VMEM((SUBLANES, wb), F32),
            pltpu.VMEM((SUBLANES, wb), F32),
            pltpu.VMEM((tm // chunk, LANES, chunk), F32),
        ],
        compiler_params=pltpu.CompilerParams(
            dimension_semantics=("arbitrary",), vmem_limit_bytes=VMEM_LIMIT),
        name="projmlstm",
    )(x2, g, w9, w_gg, w_gate, w_vt, bias_row, conv_q, conv_k, norm_b)


def _attn_kernel(slopes_ref, qi_ref, kj_ref, lq_ref, q_ref, k_ref, vt_ref, za_ref, na_ref, o_ref,
                 qs_sc, s_sc, m_sc, acc_sc, *, tq, nq, lam_init):
    h = pl.program_id(1)
    d = HEAD_DIM_A
    dv = 2 * d
    slope2 = slopes_ref[h] * LOG2E
    rows2 = 2 * tq
    n_ct = rows2 // LANES
    n_steps = nq * (nq + 1) // 2
    off_unroll = 4
    assert (n_steps - nq) % off_unroll == 0

    lane = lax.broadcasted_iota(jnp.int32, (tq, LANES), 1)
    lane2 = lax.broadcasted_iota(jnp.int32, (rows2, LANES), 1)
    ones_cols = jnp.where(lane2 < 3, 1.0, 0.0).astype(BF16)
    for qb in range(nq):
        q = (q_ref[qb * tq:(qb + 1) * tq, :].astype(F32) * (d ** -0.5 * LOG2E)).astype(BF16)
        zero = jnp.zeros_like(q)
        qs = jnp.concatenate([jnp.where(lane < d, q, zero), jnp.where(lane >= d, q, zero)], axis=0)
        qs_sc[qb] = jnp.concatenate([qs, ones_cols], axis=1)
    kpos = lax.broadcasted_iota(jnp.int32, (tq, LANES), 0)
    b_hi, b_mid, b_lo = (part.astype(F32) for part in _split3(kpos.astype(F32) * slope2))
    k_bias = jnp.where(lane == 0, b_hi, jnp.where(lane == 1, b_mid, jnp.where(lane == 2, b_lo, 0.0)))
    k_bias = k_bias.astype(BF16)
    ones_row = jnp.where(lax.broadcasted_iota(jnp.int32, (2 * SUBLANES, tq), 0) == 0, 1.0, 0.0).astype(BF16)

    lq = lq_ref[...]
    lam = (jnp.exp(jnp.sum(lq[0:1] * lq[1:2], axis=1, keepdims=True))
           - jnp.exp(jnp.sum(lq[2:3] * lq[3:4], axis=1, keepdims=True)) + lam_init)

    def scores(t, slot):
        k = k_ref[pl.ds(pl.multiple_of(kj_ref[t] * tq, tq), tq), :]
        s_sc[slot] = lax.dot_general(jnp.concatenate([k, k_bias], axis=1), qs_sc[qi_ref[t]],
                                     (((1,), (1,)), ((), ())), preferred_element_type=F32)

    def value_product(j, ps):
        vt_aug = jnp.concatenate([vt_ref[j], ones_row], axis=0)
        return jnp.dot(vt_aug, jnp.concatenate(ps, axis=1), preferred_element_type=F32)

    def first_update(i, slot):
        ps = []
        for c in range(n_ct):
            cs = slice(c * LANES, (c + 1) * LANES)
            s = jnp.where(kpos <= lane + (c * LANES) % tq, s_sc[slot, :, cs], NEG)
            m_new = jnp.max(s, axis=0, keepdims=True)
            ps.append(jnp.exp2(s - m_new).astype(BF16))
            m_sc[i, :, cs] = m_new
        acc_sc[i] = value_product(i, ps)

    def update(i, j, slot):
        off = ((j - i) * tq).astype(F32) * slope2
        ps, alphas = [], []
        for c in range(n_ct):
            cs = slice(c * LANES, (c + 1) * LANES)
            s = s_sc[slot, :, cs]
            m_prev = m_sc[i, :, cs]
            m_new = jnp.maximum(m_prev, jnp.max(s, axis=0, keepdims=True) + off)
            alphas.append(jnp.exp2(m_prev - m_new))
            ps.append(jnp.exp2(s - (m_new - off)).astype(BF16))
            m_sc[i, :, cs] = m_new
        acc_sc[i] = jnp.concatenate(alphas, axis=1) * acc_sc[i] + value_product(j, ps)

    def finalize(i, carry):
        rows = pl.ds(pl.multiple_of(i * tq, tq), tq)
        ot = acc_sc[i, :dv, :] / acc_sc[i, dv:dv + 1, :]
        od = (ot[:, :tq] - lam * ot[:, tq:]).T
        ms = jnp.mean(od * od, axis=-1, keepdims=True)
        y = od * lax.rsqrt(ms + EPS) * na_ref[...] * (1.0 - lam_init)
        za = za_ref[rows, :].astype(F32)
        o_ref[rows, :] = (y * (za * _sigmoid(za))).astype(BF16)
        return carry

    scores(0, 0)

    def diag_body(tt, carry):
        t = 2 * tt
        scores(t + 1, 1)
        first_update(qi_ref[t], 0)
        scores(t + 2, 0)
        first_update(qi_ref[t + 1], 1)
        return carry

    lax.fori_loop(0, nq // 2, diag_body, 0)

    def off_body(tt, carry):
        for u in range(0, off_unroll, 2):
            t = nq + off_unroll * tt + u
            scores(t + 1, 1)
            update(qi_ref[t], kj_ref[t], 0)
            scores(t + 2, 0)
            update(qi_ref[t + 1], kj_ref[t + 1], 1)
        return carry

    lax.fori_loop(0, (n_steps - nq) // off_unroll, off_body, 0)
    lax.fori_loop(0, nq, finalize, 0)


def _attention(p3, vt4, slopes, lq, norm_a, *, lam_init, tq=512):
    b, s, _ = p3.shape
    h = N_HEADS_A
    nq = s // tq
    nt = vt4.shape[1]
    assert vt4.shape[3] == tq and nt == nq and nq % 2 == 0 and (nq * (nq - 1) // 2) % 2 == 0
    pairs = ([(i, i) for i in range(nq)] + [(i, j) for i in range(nq) for j in range(i)] + [(0, 0)])
    qi = jnp.asarray([p[0] for p in pairs], jnp.int32)
    kj = jnp.asarray([p[1] for p in pairs], jnp.int32)
    res = lambda base: (lambda bi, hi: (bi, 0, base + hi))
    smem = pl.BlockSpec(memory_space=pltpu.SMEM)
    return pl.pallas_call(
        functools.partial(_attn_kernel, tq=tq, nq=nq, lam_init=lam_init),
        grid=(b, h),
        in_specs=[
            smem, smem, smem,
            pl.BlockSpec((4, HEAD_DIM_A), lambda bi, hi: (0, 0)),
            pl.BlockSpec((None, s, LANES), res(16)),
            pl.BlockSpec((None, s, LANES), res(20)),
            pl.BlockSpec((None, nt, LANES, tq), lambda bi, hi: (bi, 0, hi, 0)),
            pl.BlockSpec((None, s, LANES), res(24)),
            pl.BlockSpec((1, LANES), lambda bi, hi: (0, hi)),
        ],
        out_specs=pl.BlockSpec((None, s, LANES), lambda bi, hi: (bi, 0, hi)),
        out_shape=jax.ShapeDtypeStruct((b, s, h * LANES), BF16),
        scratch_shapes=[
            pltpu.VMEM((nq, 2 * tq, 2 * LANES), BF16),
            pltpu.VMEM((2, tq, 2 * tq), F32),
            pltpu.VMEM((nq, 1, 2 * tq), F32),
            pltpu.VMEM((nq, LANES + 2 * SUBLANES, 2 * tq), F32),
        ],
        compiler_params=pltpu.CompilerParams(
            dimension_semantics=("arbitrary", "arbitrary"), vmem_limit_bytes=VMEM_LIMIT),
        name="diffattn",
    )(slopes, qi, kj, lq, p3, p3, vt4, p3, norm_a)


def _merge_kernel(x_ref, ya_ref, yb_ref, ga_ref, gb_ref, wa_ref, wb_ref, wo_ref, g_ref, o_ref):
    a = jnp.dot(ya_ref[...], wa_ref[...], preferred_element_type=F32)
    b = jnp.dot(yb_ref[...], wb_ref[...], preferred_element_type=F32)
    merged = _sigmoid(ga_ref[...].astype(F32)) * a + _sigmoid(gb_ref[...].astype(F32)) * b
    out = jnp.dot(merged.astype(BF16), wo_ref[...], preferred_element_type=F32)
    ms = jnp.mean(out * out, axis=-1, keepdims=True)
    o_ref[...] = x_ref[...] + out * lax.rsqrt(ms + EPS) * g_ref[...]


def _merge(x2, ya2, yb2, p2, w_a, w_b, w_out, layer, g, *, tm=512):
    m, d = x2.shape
    wa = ya2.shape[1]
    row = lambda i: (i, 0)
    const = lambda i: (0, 0)
    return pl.pallas_call(
        _merge_kernel,
        grid=(m // tm,),
        in_specs=[
            pl.BlockSpec((tm, d), row),
            pl.BlockSpec((tm, wa), row),
            pl.BlockSpec((tm, wa), row),
            pl.BlockSpec((tm, d), lambda i: (i, 0)),
            pl.BlockSpec((tm, d), lambda i: (i, 1)),
            pl.BlockSpec((None, wa, d), lambda i: (layer, 0, 0)),
            pl.BlockSpec((None, wa, d), lambda i: (layer, 0, 0)),
            pl.BlockSpec((None, d, d), lambda i: (layer, 0, 0)),
            pl.BlockSpec((1, d), const),
        ],
        out_specs=pl.BlockSpec((tm, d), row),
        out_shape=jax.ShapeDtypeStruct((m, d), F32),
        compiler_params=pltpu.CompilerParams(
            dimension_semantics=("arbitrary",), vmem_limit_bytes=VMEM_LIMIT),
        name="merge",
    )(x2, ya2, yb2, p2, p2, w_a, w_b, w_out, g)


def kernel(x, norm_pre, norm_post, w_in, b_if, conv_qk, lambda_qk, norm_a, norm_b, w_a, w_b, w_out):
    bsz, seq, d = x.shape
    depth = w_in.shape[0]
    wa = N_HEADS_A * 2 * HEAD_DIM_A
    wb = N_HEADS_B * HEAD_DIM_B
    n9 = 4 * wa + 5 * wb
    ng = 2 * N_HEADS_B
    assert w_in.shape[2] == n9 + ng + 2 * d and d == 1024 and wa == 512 and wb == 512
    slopes = jnp.asarray(2.0 ** (-8.0 * np.arange(1, N_HEADS_A + 1) / N_HEADS_A), F32)
    w9 = w_in[:, :, :n9].astype(BF16)
    w_gg = w_in[:, :, n9 + ng:].astype(BF16)
    w_gate = jnp.pad(w_in[:, :, n9:n9 + ng], ((0, 0), (0, 0), (0, LANES - ng))).astype(BF16)
    w_vt = jnp.transpose(w_in[:, :, 2 * wa:3 * wa], (0, 2, 1)).astype(BF16)
    w_a_b, w_b_b, w_out_b = w_a.astype(BF16), w_b.astype(BF16), w_out.astype(BF16)

    x2 = x.reshape(bsz * seq, d)
    for l in range(depth):
        bias_row = jnp.pad(b_if[l], (0, LANES - ng)).reshape(1, LANES)
        lam_init = 0.8 - 0.6 * math.exp(-0.3 * l)
        p2, vt4, yb2 = _proj_mlstm(x2, norm_pre[l].reshape(1, d), w9, w_gg, w_gate, w_vt, l, bias_row,
                                   conv_qk[l][:, :wb], conv_qk[l][:, wb:], norm_b[l].reshape(1, wb),
                                   seq=seq)
        ya = _attention(p2.reshape(bsz, seq, -1), vt4, slopes, lambda_qk[l],
                        norm_a[l].reshape(1, wa), lam_init=lam_init)
        x2 = _merge(x2, ya.reshape(bsz * seq, wa), yb2, p2,
                    w_a_b, w_b_b, w_out_b, l, norm_post[l].reshape(1, d))
    return x2.reshape(bsz, seq, d)
```

```python
import functools
import math

import jax
import jax.numpy as jnp
import numpy as np
from jax import lax
from jax.experimental import pallas as pl
from jax.experimental.pallas import tpu as pltpu

F32 = jnp.float32
BF16 = jnp.bfloat16

N_HEADS_A = 4
HEAD_DIM_A = 64
N_HEADS_B = 4
HEAD_DIM_B = 128
CONV_W = 4
EPS = 1e-6

LANES = 128
SUBLANES = 8
NEG = -1e30
LOG2E = math.log2(math.e)
VMEM_LIMIT = 56 * 1024 * 1024


def _sigmoid(x):
    return jax.nn.sigmoid(x)


def _split3(x):
    hi = x.astype(BF16)
    r1 = x - hi.astype(F32)
    mid = r1.astype(BF16)
    lo = (r1 - mid.astype(F32)).astype(BF16)
    return hi, mid, lo


def _shift_rows(x, tail, sh):
    r = pltpu.roll(x, sh, 0)
    rt = pltpu.roll(tail, sh, 0)
    row = lax.broadcasted_iota(jnp.int32, tail.shape, 0)
    first = jnp.where(row < sh, rt, r[:SUBLANES])
    return jnp.concatenate([first, r[SUBLANES:]], axis=0)


def _conv_silu(xb, tail_sc, w_ref):
    x = xb.astype(F32)
    tail = tail_sc[...]
    w = w_ref[...]
    y = x * w[CONV_W - 1:CONV_W]
    for tap in range(CONV_W - 1):
        y = y + _shift_rows(x, tail, CONV_W - 1 - tap) * w[tap:tap + 1]
    tail_sc[...] = x[x.shape[0] - SUBLANES:]
    return y * _sigmoid(y)


def _mlstm_gates(r0, L, pm_sc, g_sc, bias_ref, cq_ref, ck_ref, tq_sc, tk_sc, wt_sc):
    nh, dh = N_HEADS_B, HEAD_DIM_B
    wb = nh * dh
    rows = slice(r0, r0 + L)
    q_all = _conv_silu(pm_sc[rows, 0:wb], tq_sc, cq_ref)
    k_all = _conv_silu(pm_sc[rows, wb:2 * wb], tk_sc, ck_ref) * (dh ** -0.5)

    g = g_sc[rows, :] + bias_ref[...]
    lf = jnp.minimum(g, 0.0) - jnp.log1p(jnp.exp(-jnp.abs(g)))
    rr = lax.broadcasted_iota(jnp.int32, (L, L), 0)
    cc = lax.broadcasted_iota(jnp.int32, (L, L), 1)
    causal = cc <= rr
    tri = jnp.where(causal, 1.0, 0.0).astype(BF16)
    cs = sum(jnp.dot(tri, part, preferred_element_type=F32) for part in _split3(lf))
    lane = lax.broadcasted_iota(jnp.int32, (L, LANES), 1)
    w = jnp.where(lane < nh, g, cs)
    wt_sc[...] = w.T
    return q_all, k_all, w, causal, lane


def _mlstm_head(h, r0, L, gates, pm_sc, nb_ref, y_ref, c_sc, n_sc, m_sc, wt_sc):
    nh, dh = N_HEADS_B, HEAD_DIM_B
    wb = nh * dh
    rows = slice(r0, r0 + L)
    hs = slice(h * dh, (h + 1) * dh)
    q_all, k_all, w, causal, lane = gates
    w_last = w[L - 1:L, :]
    lane1 = lax.broadcasted_iota(jnp.int32, (1, LANES), 1)

    def pick(x, ln, idx):
        return jnp.sum(jnp.where(ln == idx, x, 0.0), axis=1, keepdims=True)

    b_col = pick(w, lane, nh + h)
    i_col = pick(w, lane, h)
    b_row = wt_sc[nh + h:nh + h + 1, :]
    i_row = wt_sc[h:h + 1, :]
    b_last = pick(w_last, lane1, nh + h)
    m_prev = jnp.max(m_sc[h:h + 1, :], axis=1, keepdims=True)

    log_d = jnp.where(causal, b_col + (i_row - b_row), NEG)
    inter = b_col + m_prev
    m_row = jnp.maximum(inter, jnp.max(log_d, axis=1, keepdims=True))
    d_m = jnp.exp(log_d - m_row)
    w_inter = jnp.exp(inter - m_row)

    q = q_all[:, hs]
    k = k_all[:, hs]
    qb = q.astype(BF16)
    v = pm_sc[rows, 2 * wb + h * dh:2 * wb + (h + 1) * dh]
    s_m = lax.dot_general(qb, k.astype(BF16), (((1,), (1,)), ((), ())),
                          preferred_element_type=F32) * d_m
    c_old = c_sc[h]
    n_old = n_sc[h]
    num = (jnp.dot(s_m.astype(BF16), v, preferred_element_type=F32)
           + w_inter * jnp.dot(qb, c_old.astype(BF16), preferred_element_type=F32))
    den = (jnp.sum(s_m, axis=1, keepdims=True)
           + w_inter * jnp.sum(q * n_old, axis=1, keepdims=True))
    hval = num / jnp.maximum(jnp.abs(den), jnp.exp(-m_row))

    a_col = b_last - b_col + i_col
    m_new = jnp.maximum(b_last + m_prev, jnp.max(a_col, axis=0, keepdims=True))
    decay = jnp.exp(b_last + m_prev - m_new)
    kw = k * jnp.exp(a_col - m_new)
    c_sc[h] = decay * c_old + jnp.dot(kw.T.astype(BF16), v, preferred_element_type=F32)
    n_sc[h] = decay * n_old + jnp.sum(kw, axis=0, keepdims=True)
    m_sc[h:h + 1, :] = jnp.broadcast_to(m_new, (1, LANES))

    o = pm_sc[rows, 3 * wb + h * dh:3 * wb + (h + 1) * dh].astype(F32)
    hb = _sigmoid(o) * hval
    ms = jnp.mean(hb * hb, axis=-1, keepdims=True)
    yb = hb * lax.rsqrt(ms + EPS) * nb_ref[:, hs]
    z = pm_sc[rows, 4 * wb + h * dh:4 * wb + (h + 1) * dh].astype(F32)
    y_ref[rows, hs] = (yb * (z * _sigmoid(z))).astype(BF16)


def _proj_kernel(x_ref, g_ref, w9_ref, wgg_ref, wg_ref, wvt_ref, bias_ref, cq_ref, ck_ref, nb_ref,
                 p_ref, vt_ref, y_ref, pm_sc, g_sc, c_sc, n_sc, m_sc, tq_sc, tk_sc, wt_sc,
                 *, tn, chunk, steps_per_seq, attn_chunks, mlstm_chunks):
    @pl.when(pl.program_id(0) % steps_per_seq == 0)
    def _():
        c_sc[...] = jnp.zeros(c_sc.shape, F32)
        n_sc[...] = jnp.zeros(n_sc.shape, F32)
        m_sc[...] = jnp.zeros(m_sc.shape, F32)
        tq_sc[...] = jnp.zeros(tq_sc.shape, F32)
        tk_sc[...] = jnp.zeros(tk_sc.shape, F32)

    x = x_ref[...]
    ms = jnp.mean(x * x, axis=-1, keepdims=True)
    h = (x * lax.rsqrt(ms + EPS) * g_ref[...]).astype(BF16)
    n_chunks = x.shape[0] // chunk
    n_gg = wgg_ref.shape[1] // tn

    def project(w_ref, src):
        return jnp.dot(h, w_ref[:, src * tn:(src + 1) * tn], preferred_element_type=F32).astype(BF16)

    def to_pm(c):
        pm_sc[:, c * tn:(c + 1) * tn] = project(w9_ref, mlstm_chunks[c])

    def to_p_gg(c):
        p_ref[:, c * tn:(c + 1) * tn] = project(wgg_ref, c)

    def to_p_attn(c):
        p_ref[:, (n_gg + c) * tn:(n_gg + c + 1) * tn] = project(w9_ref, attn_chunks[c])

    def to_vt():
        vt_ref[...] = lax.dot_general(wvt_ref[...], h, (((1,), (1,)), ((), ())),
                                      preferred_element_type=F32).astype(BF16)

    fillers = ([functools.partial(to_p_gg, c) for c in range(n_gg)]
               + [functools.partial(to_p_attn, c) for c in range(len(attn_chunks))] + [to_vt])
    to_pm(0)
    to_pm(1)
    g_sc[...] = jnp.dot(h, wg_ref[...], preferred_element_type=F32)
    gates = []
    for ci in range(n_chunks):
        gates.append(_mlstm_gates(ci * chunk, chunk, pm_sc, g_sc, bias_ref, cq_ref, ck_ref,
                                  tq_sc, tk_sc, wt_sc.at[ci]))
        if 2 + ci < len(mlstm_chunks):
            to_pm(2 + ci)
    for c in range(2 + n_chunks, len(mlstm_chunks)):
        to_pm(c)
    for ci in range(n_chunks):
        for hd in range(N_HEADS_B):
            if fillers:
                fillers.pop(0)()
            _mlstm_head(hd, ci * chunk, chunk, gates[ci], pm_sc, nb_ref, y_ref,
                        c_sc, n_sc, m_sc, wt_sc.at[ci])
    for filler in fillers:
        filler()


def _proj_mlstm(x2, g, w9, w_gg, w_gate, w_vt, layer, bias_row, conv_q, conv_k, norm_b,
                *, seq, tm=512, tn=512, chunk=256):
    m, d = x2.shape
    n9 = w9.shape[2]
    ngg = w_gg.shape[2]
    wv = w_vt.shape[1]
    wb = N_HEADS_B * HEAD_DIM_B
    attn_chunks, mlstm_chunks = (0, 1, 3), (4, 5, 6, 7, 8)
    assert n9 == 9 * tn and wb == tn and wv == tn
    n_main = ngg + len(attn_chunks) * tn
    nt = seq // tm
    layer_blk = lambda i: (layer, 0, 0)
    const = lambda i: (0, 0)
    once = dict(pipeline_mode=pl.Buffered(1))
    return pl.pallas_call(
        functools.partial(_proj_kernel, tn=tn, chunk=chunk, steps_per_seq=nt,
                          attn_chunks=attn_chunks, mlstm_chunks=mlstm_chunks),
        grid=(m // tm,),
        in_specs=[
            pl.BlockSpec((tm, d), lambda i: (i, 0)),
            pl.BlockSpec((1, d), const),
            pl.BlockSpec((None, d, n9), layer_blk, **once),
            pl.BlockSpec((None, d, ngg), layer_blk, **once),
            pl.BlockSpec((None, d, LANES), layer_blk),
            pl.BlockSpec((None, wv, d), layer_blk),
            pl.BlockSpec((1, LANES), const),
            pl.BlockSpec((CONV_W, wb), const),
            pl.BlockSpec((CONV_W, wb), const),
            pl.BlockSpec((1, wb), const),
        ],
        out_specs=[
            pl.BlockSpec((tm, n_main), lambda i: (i, 0)),
            pl.BlockSpec((None, None, wv, tm), lambda i: (i // nt, i % nt, 0, 0)),
            pl.BlockSpec((tm, wb), lambda i: (i, 0)),
        ],
        out_shape=[
            jax.ShapeDtypeStruct((m, n_main), BF16),
            jax.ShapeDtypeStruct((m // seq, nt, wv, tm), BF16),
            jax.ShapeDtypeStruct((m, wb), BF16),
        ],
        scratch_shapes=[
            pltpu.VMEM((tm, len(mlstm_chunks) * tn), BF16),
            pltpu.VMEM((tm, LANES), F32),
            pltpu.VMEM((N_HEADS_B, HEAD_DIM_B, HEAD_DIM_B), F32),
            pltpu.VMEM((N_HEADS_B, 1, HEAD_DIM_B), F32),
            pltpu.VMEM((SUBLANES, LANES), F32),
            pltpu.VMEM((SUBLANES, wb), F32),
            pltpu.VMEM((SUBLANES, wb), F32),
            pltpu.VMEM((tm // chunk, LANES, chunk), F32),
        ],
        compiler_params=pltpu.CompilerParams(
            dimension_semantics=("arbitrary",), vmem_limit_bytes=VMEM_LIMIT),
        name="projmlstm",
    )(x2, g, w9, w_gg, w_gate, w_vt, bias_row, conv_q, conv_k, norm_b)


def _attn_kernel(slopes_ref, qi_ref, kj_ref, lq_ref, q_ref, k_ref, vt_ref, za_ref, na_ref, o_ref,
                 qs_sc, s_sc, m_sc, acc_sc, *, tq, nq, lam_init):
    h = pl.program_id(1)
    d = HEAD_DIM_A
    dv = 2 * d
    slope2 = slopes_ref[h] * LOG2E
    rows2 = 2 * tq
    n_ct = rows2 // LANES
    n_steps = nq * (nq + 1) // 2
    off_unroll = 4
    assert (n_steps - nq) % off_unroll == 0

    lane = lax.broadcasted_iota(jnp.int32, (tq, LANES), 1)
    lane2 = lax.broadcasted_iota(jnp.int32, (rows2, LANES), 1)
    ones_cols = jnp.where(lane2 < 3, 1.0, 0.0).astype(BF16)
    for qb in range(nq):
        q = (q_ref[qb * tq:(qb + 1) * tq, :].astype(F32) * (d ** -0.5 * LOG2E)).astype(BF16)
        zero = jnp.zeros_like(q)
        qs = jnp.concatenate([jnp.where(lane < d, q, zero), jnp.where(lane >= d, q, zero)], axis=0)
        qs_sc[qb] = jnp.concatenate([qs, ones_cols], axis=1)
    kpos = lax.broadcasted_iota(jnp.int32, (tq, LANES), 0)
    b_hi, b_mid, b_lo = (part.astype(F32) for part in _split3(kpos.astype(F32) * slope2))
    k_bias = jnp.where(lane == 0, b_hi, jnp.where(lane == 1, b_mid, jnp.where(lane == 2, b_lo, 0.0)))
    k_bias = k_bias.astype(BF16)
    ones_row = jnp.where(lax.broadcasted_iota(jnp.int32, (2 * SUBLANES, tq), 0) == 0, 1.0, 0.0).astype(BF16)

    lq = lq_ref[...]
    lam = (jnp.exp(jnp.sum(lq[0:1] * lq[1:2], axis=1, keepdims=True))
           - jnp.exp(jnp.sum(lq[2:3] * lq[3:4], axis=1, keepdims=True)) + lam_init)

    def scores(t, slot):
        k = k_ref[pl.ds(pl.multiple_of(kj_ref[t] * tq, tq), tq), :]
        s_sc[slot] = lax.dot_general(jnp.concatenate([k, k_bias], axis=1), qs_sc[qi_ref[t]],
                                     (((1,), (1,)), ((), ())), preferred_element_type=F32)

    def value_product(j, ps):
        vt_aug = jnp.concatenate([vt_ref[j], ones_row], axis=0)
        return jnp.dot(vt_aug, jnp.concatenate(ps, axis=1), preferred_element_type=F32)

    def first_update(i, slot):
        ps = []
        for c in range(n_ct):
            cs = slice(c * LANES, (c + 1) * LANES)
            s = jnp.where(kpos <= lane + (c * LANES) % tq, s_sc[slot, :, cs], NEG)
            m_new = jnp.max(s, axis=0, keepdims=True)
            ps.append(jnp.exp2(s - m_new).astype(BF16))
            m_sc[i, :, cs] = m_new
        acc_sc[i] = value_product(i, ps)

    def update(i, j, slot):
        off = ((j - i) * tq).astype(F32) * slope2
        ps, alphas = [], []
        for c in range(n_ct):
            cs = slice(c * LANES, (c + 1) * LANES)
            s = s_sc[slot, :, cs]
            m_prev = m_sc[i, :, cs]
            m_new = jnp.maximum(m_prev, jnp.max(s, axis=0, keepdims=True) + off)
            alphas.append(jnp.exp2(m_prev - m_new))
            ps.append(jnp.exp2(s - (m_new - off)).astype(BF16))
            m_sc[i, :, cs] = m_new
        acc_sc[i] = jnp.concatenate(alphas, axis=1) * acc_sc[i] + value_product(j, ps)

    def finalize(i, carry):
        rows = pl.ds(pl.multiple_of(i * tq, tq), tq)
        ot = acc_sc[i, :dv, :] / acc_sc[i, dv:dv + 1, :]
        od = (ot[:, :tq] - lam * ot[:, tq:]).T
        ms = jnp.mean(od * od, axis=-1, keepdims=True)
        y = od * lax.rsqrt(ms + EPS) * na_ref[...] * (1.0 - lam_init)
        za = za_ref[rows, :].astype(F32)
        o_ref[rows, :] = (y * (za * _sigmoid(za))).astype(BF16)
        return carry

    scores(0, 0)

    def diag_body(tt, carry):
        t = 2 * tt
        scores(t + 1, 1)
        first_update(qi_ref[t], 0)
        scores(t + 2, 0)
        first_update(qi_ref[t + 1], 1)
        return carry

    lax.fori_loop(0, nq // 2, diag_body, 0)

    n_trips = (n_steps - nq) // off_unroll
    assert n_trips == nq - 1 and off_unroll >= (nq - 1) / 2

    def off_body(tt, carry):
        for u in range(0, off_unroll, 2):
            t = nq + off_unroll * tt + u
            scores(t + 1, 1)
            update(qi_ref[t], kj_ref[t], 0)
            if u == 0:
                finalize(tt, carry)
            scores(t + 2, 0)
            update(qi_ref[t + 1], kj_ref[t + 1], 1)
        return carry

    lax.fori_loop(0, n_trips, off_body, 0)
    finalize(jnp.int32(nq - 1), 0)


def _attention(p3, vt4, slopes, lq, norm_a, *, lam_init, tq=512):
    b, s, _ = p3.shape
    h = N_HEADS_A
    nq = s // tq
    nt = vt4.shape[1]
    assert vt4.shape[3] == tq and nt == nq and nq % 2 == 0 and (nq * (nq - 1) // 2) % 2 == 0
    pairs = ([(i, i) for i in range(nq)] + [(i, j) for i in range(nq) for j in range(i)] + [(0, 0)])
    qi = jnp.asarray([p[0] for p in pairs], jnp.int32)
    kj = jnp.asarray([p[1] for p in pairs], jnp.int32)
    res = lambda base: (lambda bi, hi: (bi, 0, base + hi))
    smem = pl.BlockSpec(memory_space=pltpu.SMEM)
    return pl.pallas_call(
        functools.partial(_attn_kernel, tq=tq, nq=nq, lam_init=lam_init),
        grid=(b, h),
        in_specs=[
            smem, smem, smem,
            pl.BlockSpec((4, HEAD_DIM_A), lambda bi, hi: (0, 0)),
            pl.BlockSpec((None, s, LANES), res(16)),
            pl.BlockSpec((None, s, LANES), res(20)),
            pl.BlockSpec((None, nt, LANES, tq), lambda bi, hi: (bi, 0, hi, 0)),
            pl.BlockSpec((None, s, LANES), res(24)),
            pl.BlockSpec((1, LANES), lambda bi, hi: (0, hi)),
        ],
        out_specs=pl.BlockSpec((None, s, LANES), lambda bi, hi: (bi, 0, hi)),
        out_shape=jax.ShapeDtypeStruct((b, s, h * LANES), BF16),
        scratch_shapes=[
            pltpu.VMEM((nq, 2 * tq, 2 * LANES), BF16),
            pltpu.VMEM((2, tq, 2 * tq), F32),
            pltpu.VMEM((nq, 1, 2 * tq), F32),
            pltpu.VMEM((nq, LANES + 2 * SUBLANES, 2 * tq), F32),
        ],
        compiler_params=pltpu.CompilerParams(
            dimension_semantics=("arbitrary", "arbitrary"), vmem_limit_bytes=VMEM_LIMIT),
        name="diffattn",
    )(slopes, qi, kj, lq, p3, p3, vt4, p3, norm_a)


def _merge_kernel(x_ref, ya_ref, yb_ref, ga_ref, gb_ref, wa_ref, wb_ref, wo_ref, g_ref, o_ref):
    a = jnp.dot(ya_ref[...], wa_ref[...], preferred_element_type=F32)
    b = jnp.dot(yb_ref[...], wb_ref[...], preferred_element_type=F32)
    merged = _sigmoid(ga_ref[...].astype(F32)) * a + _sigmoid(gb_ref[...].astype(F32)) * b
    out = jnp.dot(merged.astype(BF16), wo_ref[...], preferred_element_type=F32)
    ms = jnp.mean(out * out, axis=-1, keepdims=True)
    o_ref[...] = x_ref[...] + out * lax.rsqrt(ms + EPS) * g_ref[...]


def _merge(x2, ya2, yb2, p2, w_a, w_b, w_out, layer, g, *, tm=512):
    m, d = x2.shape
    wa = ya2.shape[1]
    row = lambda i: (i, 0)
    const = lambda i: (0, 0)
    return pl.pallas_call(
        _merge_kernel,
        grid=(m // tm,),
        in_specs=[
            pl.BlockSpec((tm, d), row),
            pl.BlockSpec((tm, wa), row),
            pl.BlockSpec((tm, wa), row),
            pl.BlockSpec((tm, d), lambda i: (i, 0)),
            pl.BlockSpec((tm, d), lambda i: (i, 1)),
            pl.BlockSpec((None, wa, d), lambda i: (layer, 0, 0)),
            pl.BlockSpec((None, wa, d), lambda i: (layer, 0, 0)),
            pl.BlockSpec((None, d, d), lambda i: (layer, 0, 0)),
            pl.BlockSpec((1, d), const),
        ],
        out_specs=pl.BlockSpec((tm, d), row),
        out_shape=jax.ShapeDtypeStruct((m, d), F32),
        compiler_params=pltpu.CompilerParams(
            dimension_semantics=("arbitrary",), vmem_limit_bytes=VMEM_LIMIT),
        name="merge",
    )(x2, ya2, yb2, p2, p2, w_a, w_b, w_out, g)


def _cast_kernel(w_ref, o_ref):
    o_ref[...] = w_ref[...].astype(BF16)


def _cast_t_kernel(w_ref, o_ref):
    o_ref[...] = w_ref[...].T.astype(BF16)


def _realign_kernel(lo_ref, hi_ref, o_ref, gate_ref, *, shift):
    tn = o_ref.shape[1]
    both = jnp.concatenate([lo_ref[...], hi_ref[...]], axis=1)
    o_ref[...] = pltpu.roll(both, 2 * tn - shift, 1)[:, :tn].astype(BF16)

    @pl.when(pl.program_id(1) == 0)
    def _():
        head = lo_ref[:, :LANES]
        lane = lax.broadcasted_iota(jnp.int32, head.shape, 1)
        gate_ref[...] = jnp.where(lane < shift, head, 0.0).astype(BF16)


def _prep_weights(w_in, n9, ng, wa, *, tn=512):
    depth, d, n_in = w_in.shape
    ngg = n_in - n9 - ng
    assert n9 % tn == 0 and ngg % tn == 0 and ng < LANES and (2 * wa) % tn == 0 and wa == tn
    params = pltpu.CompilerParams(dimension_semantics=("arbitrary", "arbitrary"),
                                  vmem_limit_bytes=VMEM_LIMIT)
    w9 = pl.pallas_call(
        _cast_kernel,
        grid=(depth, n9 // tn),
        in_specs=[pl.BlockSpec((None, d, tn), lambda l, c: (l, 0, c))],
        out_specs=pl.BlockSpec((None, d, tn), lambda l, c: (l, 0, c)),
        out_shape=jax.ShapeDtypeStruct((depth, d, n9), BF16),
        compiler_params=params, name="wcast",
    )(w_in)
    va_chunk = 2 * wa // tn
    w_vt = pl.pallas_call(
        _cast_t_kernel,
        grid=(depth, 1),
        in_specs=[pl.BlockSpec((None, d, tn), lambda l, c: (l, 0, va_chunk))],
        out_specs=pl.BlockSpec((None, tn, d), lambda l, c: (l, 0, 0)),
        out_shape=jax.ShapeDtypeStruct((depth, tn, d), BF16),
        compiler_params=params, name="wcast_t",
    )(w_in)
    first = n9 // tn
    w_gg, w_gate = pl.pallas_call(
        functools.partial(_realign_kernel, shift=ng),
        grid=(depth, ngg // tn),
        in_specs=[pl.BlockSpec((None, d, tn), lambda l, c: (l, 0, first + c)),
                  pl.BlockSpec((None, d, tn), lambda l, c: (l, 0, first + c + 1))],
        out_specs=[pl.BlockSpec((None, d, tn), lambda l, c: (l, 0, c)),
                   pl.BlockSpec((None, d, LANES), lambda l, c: (l, 0, 0))],
        out_shape=[jax.ShapeDtypeStruct((depth, d, ngg), BF16),
                   jax.ShapeDtypeStruct((depth, d, LANES), BF16)],
        compiler_params=params, name="wrealign",
    )(w_in, w_in)
    return w9, w_gg, w_gate, w_vt


def kernel(x, norm_pre, norm_post, w_in, b_if, conv_qk, lambda_qk, norm_a, norm_b, w_a, w_b, w_out):
    bsz, seq, d = x.shape
    depth = w_in.shape[0]
    wa = N_HEADS_A * 2 * HEAD_DIM_A
    wb = N_HEADS_B * HEAD_DIM_B
    n9 = 4 * wa + 5 * wb
    ng = 2 * N_HEADS_B
    assert w_in.shape[2] == n9 + ng + 2 * d and d == 1024 and wa == 512 and wb == 512
    slopes = jnp.asarray(2.0 ** (-8.0 * np.arange(1, N_HEADS_A + 1) / N_HEADS_A), F32)
    w9, w_gg, w_gate, w_vt = _prep_weights(w_in, n9, ng, wa)
    w_a_b, w_b_b, w_out_b = w_a.astype(BF16), w_b.astype(BF16), w_out.astype(BF16)

    x2 = x.reshape(bsz * seq, d)
    for l in range(depth):
        bias_row = jnp.pad(b_if[l], (0, LANES - ng)).reshape(1, LANES)
        lam_init = 0.8 - 0.6 * math.exp(-0.3 * l)
        p2, vt4, yb2 = _proj_mlstm(x2, norm_pre[l].reshape(1, d), w9, w_gg, w_gate, w_vt, l, bias_row,
                                   conv_qk[l][:, :wb], conv_qk[l][:, wb:], norm_b[l].reshape(1, wb),
                                   seq=seq)
        ya = _attention(p2.reshape(bsz, seq, -1), vt4, slopes, lambda_qk[l],
                        norm_a[l].reshape(1, wa), lam_init=lam_init)
        x2 = _merge(x2, ya.reshape(bsz * seq, wa), yb2, p2,
                    w_a_b, w_b_b, w_out_b, l, norm_post[l].reshape(1, d))
    return x2.reshape(bsz, seq, d)
```

```python
import functools
import math

import jax
import jax.numpy as jnp
import numpy as np
from jax import lax
from jax.experimental import pallas as pl
from jax.experimental.pallas import tpu as pltpu

F32 = jnp.float32
BF16 = jnp.bfloat16

N_HEADS_A = 4
HEAD_DIM_A = 64
N_HEADS_B = 4
HEAD_DIM_B = 128
CONV_W = 4
EPS = 1e-6

LANES = 128
SUBLANES = 8
NEG = -1e30
LOG2E = math.log2(math.e)
VMEM_LIMIT = 56 * 1024 * 1024


def _sigmoid(x):
    return jax.nn.sigmoid(x)


def _split3(x):
    hi = x.astype(BF16)
    r1 = x - hi.astype(F32)
    mid = r1.astype(BF16)
    lo = (r1 - mid.astype(F32)).astype(BF16)
    return hi, mid, lo


def _shift_rows(x, tail, sh):
    r = pltpu.roll(x, sh, 0)
    rt = pltpu.roll(tail, sh, 0)
    row = lax.broadcasted_iota(jnp.int32, tail.shape, 0)
    first = jnp.where(row < sh, rt, r[:SUBLANES])
    return jnp.concatenate([first, r[SUBLANES:]], axis=0)


def _conv_silu(xb, tail_sc, w_ref):
    x = xb.astype(F32)
    tail = tail_sc[...]
    w = w_ref[...]
    y = x * w[CONV_W - 1:CONV_W]
    for tap in range(CONV_W - 1):
        y = y + _shift_rows(x, tail, CONV_W - 1 - tap) * w[tap:tap + 1]
    tail_sc[...] = x[x.shape[0] - SUBLANES:]
    return y * _sigmoid(y)


def _mlstm_gates(r0, L, pm_sc, g_sc, bias_ref, cq_ref, ck_ref, tq_sc, tk_sc, wt_sc):
    nh, dh = N_HEADS_B, HEAD_DIM_B
    wb = nh * dh
    rows = slice(r0, r0 + L)
    q_all = _conv_silu(pm_sc[rows, 0:wb], tq_sc, cq_ref)
    k_all = _conv_silu(pm_sc[rows, wb:2 * wb], tk_sc, ck_ref) * (dh ** -0.5)

    g = g_sc[rows, :] + bias_ref[...]
    lf = jnp.minimum(g, 0.0) - jnp.log1p(jnp.exp(-jnp.abs(g)))
    rr = lax.broadcasted_iota(jnp.int32, (L, L), 0)
    cc = lax.broadcasted_iota(jnp.int32, (L, L), 1)
    causal = cc <= rr
    tri = jnp.where(causal, 1.0, 0.0).astype(BF16)
    cs = sum(jnp.dot(tri, part, preferred_element_type=F32) for part in _split3(lf))
    lane = lax.broadcasted_iota(jnp.int32, (L, LANES), 1)
    w = jnp.where(lane < nh, g, cs)
    wt_sc[...] = w.T
    return q_all, k_all, w, causal, lane


def _mlstm_head(h, r0, L, gates, pm_sc, nb_ref, y_ref, c_sc, n_sc, m_sc, wt_sc):
    nh, dh = N_HEADS_B, HEAD_DIM_B
    wb = nh * dh
    rows = slice(r0, r0 + L)
    hs = slice(h * dh, (h + 1) * dh)
    q_all, k_all, w, causal, lane = gates
    w_last = w[L - 1:L, :]
    lane1 = lax.broadcasted_iota(jnp.int32, (1, LANES), 1)

    def pick(x, ln, idx):
        return jnp.sum(jnp.where(ln == idx, x, 0.0), axis=1, keepdims=True)

    b_col = pick(w, lane, nh + h)
    i_col = pick(w, lane, h)
    b_row = wt_sc[nh + h:nh + h + 1, :]
    i_row = wt_sc[h:h + 1, :]
    b_last = pick(w_last, lane1, nh + h)
    m_prev = jnp.max(m_sc[h:h + 1, :], axis=1, keepdims=True)

    log_d = jnp.where(causal, b_col + (i_row - b_row), NEG)
    inter = b_col + m_prev
    m_row = jnp.maximum(inter, jnp.max(log_d, axis=1, keepdims=True))
    d_m = jnp.exp(log_d - m_row)
    w_inter = jnp.exp(inter - m_row)

    q = q_all[:, hs]
    k = k_all[:, hs]
    qb = q.astype(BF16)
    v = pm_sc[rows, 2 * wb + h * dh:2 * wb + (h + 1) * dh]
    s_m = lax.dot_general(qb, k.astype(BF16), (((1,), (1,)), ((), ())),
                          preferred_element_type=F32) * d_m
    c_old = c_sc[h]
    n_old = n_sc[h]
    num = (jnp.dot(s_m.astype(BF16), v, preferred_element_type=F32)
           + w_inter * jnp.dot(qb, c_old.astype(BF16), preferred_element_type=F32))
    den = (jnp.sum(s_m, axis=1, keepdims=True)
           + w_inter * jnp.sum(q * n_old, axis=1, keepdims=True))
    hval = num / jnp.maximum(jnp.abs(den), jnp.exp(-m_row))

    a_col = b_last - b_col + i_col
    m_new = jnp.maximum(b_last + m_prev, jnp.max(a_col, axis=0, keepdims=True))
    decay = jnp.exp(b_last + m_prev - m_new)
    kw = k * jnp.exp(a_col - m_new)
    c_sc[h] = decay * c_old + jnp.dot(kw.T.astype(BF16), v, preferred_element_type=F32)
    n_sc[h] = decay * n_old + jnp.sum(kw, axis=0, keepdims=True)
    m_sc[h:h + 1, :] = jnp.broadcast_to(m_new, (1, LANES))

    o = pm_sc[rows, 3 * wb + h * dh:3 * wb + (h + 1) * dh].astype(F32)
    hb = _sigmoid(o) * hval
    ms = jnp.mean(hb * hb, axis=-1, keepdims=True)
    yb = hb * lax.rsqrt(ms + EPS) * nb_ref[:, hs]
    z = pm_sc[rows, 4 * wb + h * dh:4 * wb + (h + 1) * dh].astype(F32)
    y_ref[rows, hs] = (yb * (z * _sigmoid(z))).astype(BF16)


def _proj_kernel(x_ref, g_ref, w9_ref, wgg_ref, wvt_ref, bias_ref, cq_ref, ck_ref, nb_ref,
                 pg_ref, pa_ref, vt_ref, y_ref, pm_sc, g_sc, c_sc, n_sc, m_sc, tq_sc, tk_sc, wt_sc,
                 *, tn, chunk, steps_per_seq, attn_chunks, mlstm_chunks):
    @pl.when(pl.program_id(0) % steps_per_seq == 0)
    def _():
        c_sc[...] = jnp.zeros(c_sc.shape, F32)
        n_sc[...] = jnp.zeros(n_sc.shape, F32)
        m_sc[...] = jnp.zeros(m_sc.shape, F32)
        tq_sc[...] = jnp.zeros(tq_sc.shape, F32)
        tk_sc[...] = jnp.zeros(tk_sc.shape, F32)

    x = x_ref[...]
    ms = jnp.mean(x * x, axis=-1, keepdims=True)
    h = (x * lax.rsqrt(ms + EPS) * g_ref[...]).astype(BF16)
    n_chunks = x.shape[0] // chunk
    n_gg = pg_ref.shape[1] // tn

    def project(w_ref, src):
        return jnp.dot(h, w_ref[:, src * tn:(src + 1) * tn], preferred_element_type=F32).astype(BF16)

    def to_pm(c):
        pm_sc[:, c * tn:(c + 1) * tn] = project(w9_ref, mlstm_chunks[c])

    def to_p_gg(c):
        pg_ref[:, c * tn:(c + 1) * tn] = project(wgg_ref, c)

    def to_p_attn(c):
        res = project(w9_ref, attn_chunks[c])
        per = tn // LANES
        for hd in range(per):
            pa_ref[c * per + hd] = res[:, hd * LANES:(hd + 1) * LANES]

    def to_vt():
        vt_ref[...] = lax.dot_general(wvt_ref[...], h, (((1,), (1,)), ((), ())),
                                      preferred_element_type=F32).astype(BF16)

    fillers = ([functools.partial(to_p_gg, c) for c in range(n_gg)]
               + [functools.partial(to_p_attn, c) for c in range(len(attn_chunks))] + [to_vt])
    to_pm(0)
    to_pm(1)
    g_sc[...] = jnp.dot(h, wgg_ref[:, n_gg * tn:n_gg * tn + LANES], preferred_element_type=F32)
    gates = []
    for ci in range(n_chunks):
        gates.append(_mlstm_gates(ci * chunk, chunk, pm_sc, g_sc, bias_ref, cq_ref, ck_ref,
                                  tq_sc, tk_sc, wt_sc.at[ci]))
        if 2 + ci < len(mlstm_chunks):
            to_pm(2 + ci)
    for c in range(2 + n_chunks, len(mlstm_chunks)):
        to_pm(c)
    for ci in range(n_chunks):
        for hd in range(N_HEADS_B):
            if fillers:
                fillers.pop(0)()
            _mlstm_head(hd, ci * chunk, chunk, gates[ci], pm_sc, nb_ref, y_ref,
                        c_sc, n_sc, m_sc, wt_sc.at[ci])
    for filler in fillers:
        filler()


def _proj_mlstm(x2, g, w_all, w_vt, layer, n9, ngg, bias_row, conv_q, conv_k, norm_b,
                *, seq, tm=512, tn=512, chunk=256):
    m, d = x2.shape
    wv = w_vt.shape[1]
    wb = N_HEADS_B * HEAD_DIM_B
    tail = w_all.shape[2] - n9
    assert n9 % tail == 0 and tail >= ngg + LANES
    attn_chunks, mlstm_chunks = (0, 1, 3), (4, 5, 6, 7, 8)
    assert n9 == 9 * tn and wb == tn and wv == tn
    n_slabs = len(attn_chunks) * tn // LANES
    nt = seq // tm
    layer_blk = lambda i: (layer, 0, 0)
    const = lambda i: (0, 0)
    once = dict(pipeline_mode=pl.Buffered(1))
    return pl.pallas_call(
        functools.partial(_proj_kernel, tn=tn, chunk=chunk, steps_per_seq=nt,
                          attn_chunks=attn_chunks, mlstm_chunks=mlstm_chunks),
        grid=(m // tm,),
        in_specs=[
            pl.BlockSpec((tm, d), lambda i: (i, 0)),
            pl.BlockSpec((1, d), const),
            pl.BlockSpec((None, d, n9), layer_blk, **once),
            pl.BlockSpec((None, d, tail), lambda i: (layer, 0, n9 // tail), **once),
            pl.BlockSpec((None, wv, d), layer_blk),
            pl.BlockSpec((1, LANES), const),
            pl.BlockSpec((CONV_W, wb), const),
            pl.BlockSpec((CONV_W, wb), const),
            pl.BlockSpec((1, wb), const),
        ],
        out_specs=[
            pl.BlockSpec((tm, ngg), lambda i: (i, 0)),
            pl.BlockSpec((None, n_slabs, tm, LANES), lambda i: (i // nt, 0, i % nt, 0)),
            pl.BlockSpec((None, None, wv, tm), lambda i: (i // nt, i % nt, 0, 0)),
            pl.BlockSpec((tm, wb), lambda i: (i, 0)),
        ],
        out_shape=[
            jax.ShapeDtypeStruct((m, ngg), BF16),
            jax.ShapeDtypeStruct((m // seq, n_slabs, seq, LANES), BF16),
            jax.ShapeDtypeStruct((m // seq, nt, wv, tm), BF16),
            jax.ShapeDtypeStruct((m, wb), BF16),
        ],
        scratch_shapes=[
            pltpu.VMEM((tm, len(mlstm_chunks) * tn), BF16),
            pltpu.VMEM((tm, LANES), F32),
            pltpu.VMEM((N_HEADS_B, HEAD_DIM_B, HEAD_DIM_B), F32),
            pltpu.VMEM((N_HEADS_B, 1, HEAD_DIM_B), F32),
            pltpu.VMEM((SUBLANES, LANES), F32),
            pltpu.VMEM((SUBLANES, wb), F32),
            pltpu.VMEM((SUBLANES, wb), F32),
            pltpu.VMEM((tm // chunk, LANES, chunk), F32),
        ],
        compiler_params=pltpu.CompilerParams(
            dimension_semantics=("arbitrary",), vmem_limit_bytes=VMEM_LIMIT),
        name="projmlstm",
    )(x2, g, w_all, w_all, w_vt, bias_row, conv_q, conv_k, norm_b)


def _attn_kernel(slopes_ref, qi_ref, kj_ref, lq_ref, q_ref, k_ref, vt_ref, za_ref, na_ref, o_ref,
                 qs_sc, s_sc, m_sc, acc_sc, *, tq, nq, lam_init):
    h = pl.program_id(1)
    d = HEAD_DIM_A
    dv = 2 * d
    slope2 = slopes_ref[h] * LOG2E
    rows2 = 2 * tq
    n_ct = rows2 // LANES
    n_steps = nq * (nq + 1) // 2

    lane = lax.broadcasted_iota(jnp.int32, (tq, LANES), 1)
    lane2 = lax.broadcasted_iota(jnp.int32, (rows2, LANES), 1)
    ones_cols = jnp.where(lane2 < 3, 1.0, 0.0).astype(BF16)
    for qb in range(nq):
        q = (q_ref[qb * tq:(qb + 1) * tq, :].astype(F32) * (d ** -0.5 * LOG2E)).astype(BF16)
        zero = jnp.zeros_like(q)
        qs = jnp.concatenate([jnp.where(lane < d, q, zero), jnp.where(lane >= d, q, zero)], axis=0)
        qs_sc[qb] = jnp.concatenate([qs, ones_cols], axis=1)
    kpos = lax.broadcasted_iota(jnp.int32, (tq, LANES), 0)
    b_hi, b_mid, b_lo = (part.astype(F32) for part in _split3(kpos.astype(F32) * slope2))
    k_bias = jnp.where(lane == 0, b_hi, jnp.where(lane == 1, b_mid, jnp.where(lane == 2, b_lo, 0.0)))
    k_bias = k_bias.astype(BF16)
    ones_row = jnp.where(lax.broadcasted_iota(jnp.int32, (2 * SUBLANES, tq), 0) == 0, 1.0, 0.0).astype(BF16)

    lq = lq_ref[...]
    lam = (jnp.exp(jnp.sum(lq[0:1] * lq[1:2], axis=1, keepdims=True))
           - jnp.exp(jnp.sum(lq[2:3] * lq[3:4], axis=1, keepdims=True)) + lam_init)

    def scores(t, slot):
        k = k_ref[pl.ds(pl.multiple_of(kj_ref[t] * tq, tq), tq), :]
        s = lax.dot_general(jnp.concatenate([k, k_bias], axis=1), qs_sc[qi_ref[t]],
                            (((1,), (1,)), ((), ())), preferred_element_type=F32)
        for c in range(n_ct):
            s_sc[slot, c] = s[:, c * LANES:(c + 1) * LANES]

    group = 2

    def value_product(j, ps):
        vt_aug = jnp.concatenate([vt_ref[j], ones_row], axis=0)
        return jnp.dot(vt_aug, jnp.concatenate(ps, axis=1), preferred_element_type=F32)

    def first_update(i, slot):
        for c0 in range(0, n_ct, group):
            ps = []
            for c in range(c0, c0 + group):
                cs = slice(c * LANES, (c + 1) * LANES)
                s = jnp.where(kpos <= lane + (c * LANES) % tq, s_sc[slot, c], NEG)
                m_new = jnp.max(s, axis=0, keepdims=True)
                ps.append(jnp.exp2(s - m_new).astype(BF16))
                m_sc[i, :, cs] = m_new
            acc_sc[i, :, c0 * LANES:(c0 + group) * LANES] = value_product(i, ps)

    def update(i, j, slot):
        off = ((j - i) * tq).astype(F32) * slope2
        for c0 in range(0, n_ct, group):
            gs = slice(c0 * LANES, (c0 + group) * LANES)
            ps, alphas = [], []
            for c in range(c0, c0 + group):
                cs = slice(c * LANES, (c + 1) * LANES)
                s = s_sc[slot, c]
                m_prev = m_sc[i, :, cs]
                m_new = jnp.maximum(m_prev, jnp.max(s, axis=0, keepdims=True) + off)
                alphas.append(jnp.exp2(m_prev - m_new))
                ps.append(jnp.exp2(s - (m_new - off)).astype(BF16))
                m_sc[i, :, cs] = m_new
            acc_sc[i, :, gs] = jnp.concatenate(alphas, axis=1) * acc_sc[i, :, gs] + value_product(j, ps)

    def finalize(i, carry):
        rows = pl.ds(pl.multiple_of(i * tq, tq), tq)
        ot = acc_sc[i, :dv, :] / acc_sc[i, dv:dv + 1, :]
        od = (ot[:, :tq] - lam * ot[:, tq:]).T
        ms = jnp.mean(od * od, axis=-1, keepdims=True)
        y = od * lax.rsqrt(ms + EPS) * na_ref[...] * (1.0 - lam_init)
        za = za_ref[rows, :].astype(F32)
        o_ref[rows, :] = (y * (za * _sigmoid(za))).astype(BF16)
        return carry

    unroll = s_sc.shape[0]
    assert nq % unroll == 0
    scores(0, 0)

    def diag_body(tt, carry):
        for u in range(unroll):
            t = unroll * tt + u
            scores(t + 1, (u + 1) % unroll)
            first_update(qi_ref[t], u)
        return carry

    lax.fori_loop(0, nq // unroll, diag_body, 0)

    n_trips = (n_steps - nq) // unroll
    assert (n_steps - nq) % unroll == 0 and n_trips == nq - 1 and unroll >= (nq - 1) / 2

    def off_body(tt, carry):
        for u in range(unroll):
            t = nq + unroll * tt + u
            scores(t + 1, (u + 1) % unroll)
            update(qi_ref[t], kj_ref[t], u)
            if u == 0:
                finalize(tt, carry)
        return carry

    lax.fori_loop(0, n_trips, off_body, 0)
    finalize(jnp.int32(nq - 1), 0)


def _attention(pa, vt4, slopes, lq, norm_a, *, lam_init, tq=512):
    b, _, s, _ = pa.shape
    h = N_HEADS_A
    nq = s // tq
    nt = vt4.shape[1]
    assert vt4.shape[3] == tq and nt == nq and nq % 2 == 0 and (nq * (nq - 1) // 2) % 2 == 0
    pairs = ([(i, i) for i in range(nq)] + [(i, j) for i in range(nq) for j in range(i)] + [(0, 0)])
    qi = jnp.asarray([p[0] for p in pairs], jnp.int32)
    kj = jnp.asarray([p[1] for p in pairs], jnp.int32)
    slab = lambda base: (lambda bi, hi: (bi, base + hi, 0, 0))
    smem = pl.BlockSpec(memory_space=pltpu.SMEM)
    return pl.pallas_call(
        functools.partial(_attn_kernel, tq=tq, nq=nq, lam_init=lam_init),
        grid=(b, h),
        in_specs=[
            smem, smem, smem,
            pl.BlockSpec((4, HEAD_DIM_A), lambda bi, hi: (0, 0)),
            pl.BlockSpec((None, None, s, LANES), slab(0)),
            pl.BlockSpec((None, None, s, LANES), slab(h)),
            pl.BlockSpec((None, nt, LANES, tq), lambda bi, hi: (bi, 0, hi, 0)),
            pl.BlockSpec((None, None, s, LANES), slab(2 * h)),
            pl.BlockSpec((1, LANES), lambda bi, hi: (0, hi)),
        ],
        out_specs=pl.BlockSpec((None, None, s, LANES), lambda bi, hi: (bi, hi, 0, 0)),
        out_shape=jax.ShapeDtypeStruct((b, h, s, LANES), BF16),
        scratch_shapes=[
            pltpu.VMEM((nq, 2 * tq, 2 * LANES), BF16),
            pltpu.VMEM((4, 2 * tq // LANES, tq, LANES), F32),
            pltpu.VMEM((nq, 1, 2 * tq), F32),
            pltpu.VMEM((nq, LANES + 2 * SUBLANES, 2 * tq), F32),
        ],
        compiler_params=pltpu.CompilerParams(
            dimension_semantics=("arbitrary", "arbitrary"), vmem_limit_bytes=VMEM_LIMIT),
        name="diffattn",
    )(slopes, qi, kj, lq, pa, pa, vt4, pa, norm_a)


def _merge_kernel(x_ref, ya_ref, yb_ref, ga_ref, gb_ref, wa_ref, wb_ref, wo_ref, g_ref, o_ref):
    ya = jnp.concatenate([ya_ref[hd] for hd in range(ya_ref.shape[0])], axis=1)
    a = jnp.dot(ya, wa_ref[...], preferred_element_type=F32)
    b = jnp.dot(yb_ref[...], wb_ref[...], preferred_element_type=F32)
    merged = _sigmoid(ga_ref[...].astype(F32)) * a + _sigmoid(gb_ref[...].astype(F32)) * b
    out = jnp.dot(merged.astype(BF16), wo_ref[...], preferred_element_type=F32)
    ms = jnp.mean(out * out, axis=-1, keepdims=True)
    o_ref[...] = x_ref[...] + out * lax.rsqrt(ms + EPS) * g_ref[...]


def _merge(x2, ya4, yb2, pg, w_a, w_b, w_out, layer, g, *, tm=512):
    m, d = x2.shape
    _, ha, seq, _ = ya4.shape
    wa = yb2.shape[1]
    nt = seq // tm
    row = lambda i: (i, 0)
    const = lambda i: (0, 0)
    return pl.pallas_call(
        _merge_kernel,
        grid=(m // tm,),
        in_specs=[
            pl.BlockSpec((tm, d), row),
            pl.BlockSpec((None, ha, tm, LANES), lambda i: (i // nt, 0, i % nt, 0)),
            pl.BlockSpec((tm, wa), row),
            pl.BlockSpec((tm, d), lambda i: (i, 0)),
            pl.BlockSpec((tm, d), lambda i: (i, 1)),
            pl.BlockSpec((None, wa, d), lambda i: (layer, 0, 0)),
            pl.BlockSpec((None, wa, d), lambda i: (layer, 0, 0)),
            pl.BlockSpec((None, d, d), lambda i: (layer, 0, 0)),
            pl.BlockSpec((1, d), const),
        ],
        out_specs=pl.BlockSpec((tm, d), row),
        out_shape=jax.ShapeDtypeStruct((m, d), F32),
        compiler_params=pltpu.CompilerParams(
            dimension_semantics=("arbitrary",), vmem_limit_bytes=VMEM_LIMIT),
        name="merge",
    )(x2, ya4, yb2, pg, pg, w_a, w_b, w_out, g)


def kernel(x, norm_pre, norm_post, w_in, b_if, conv_qk, lambda_qk, norm_a, norm_b, w_a, w_b, w_out):
    bsz, seq, d = x.shape
    depth = w_in.shape[0]
    wa = N_HEADS_A * 2 * HEAD_DIM_A
    wb = N_HEADS_B * HEAD_DIM_B
    n9 = 4 * wa + 5 * wb
    ng = 2 * N_HEADS_B
    assert w_in.shape[2] == n9 + ng + 2 * d and d == 1024 and wa == 512 and wb == 512
    slopes = jnp.asarray(2.0 ** (-8.0 * np.arange(1, N_HEADS_A + 1) / N_HEADS_A), F32)
    pad = n9 // 2 - 2 * d - ng
    w_all = jnp.concatenate([w_in[:, :, :n9], w_in[:, :, n9 + ng:], w_in[:, :, n9:n9 + ng],
                             jnp.zeros((depth, d, pad), w_in.dtype)], axis=-1).astype(BF16)
    w_vt = jnp.transpose(w_in[:, :, 2 * wa:3 * wa], (0, 2, 1)).astype(BF16)
    w_a_b, w_b_b, w_out_b = w_a.astype(BF16), w_b.astype(BF16), w_out.astype(BF16)

    x2 = x.reshape(bsz * seq, d)
    for l in range(depth):
        bias_row = jnp.pad(b_if[l], (0, LANES - ng)).reshape(1, LANES)
        lam_init = 0.8 - 0.6 * math.exp(-0.3 * l)
        pg, pa, vt4, yb2 = _proj_mlstm(x2, norm_pre[l].reshape(1, d), w_all, w_vt, l, n9, 2 * d, bias_row,
                                   conv_qk[l][:, :wb], conv_qk[l][:, wb:], norm_b[l].reshape(1, wb),
                                   seq=seq)
        ya = _attention(pa, vt4, slopes, lambda_qk[l], norm_a[l].reshape(1, wa), lam_init=lam_init)
        x2 = _merge(x2, ya, yb2, pg,
                    w_a_b, w_b_b, w_out_b, l, norm_post[l].reshape(1, d))
    return x2.reshape(bsz, seq, d)
```

```python
import functools
import math

import jax
import jax.numpy as jnp
import numpy as np
from jax import lax
from jax.experimental import pallas as pl
from jax.experimental.pallas import tpu as pltpu

F32 = jnp.float32
BF16 = jnp.bfloat16

N_HEADS_A = 4
HEAD_DIM_A = 64
N_HEADS_B = 4
HEAD_DIM_B = 128
CONV_W = 4
EPS = 1e-6

LANES = 128
SUBLANES = 8
NEG = -1e30
LOG2E = math.log2(math.e)
VMEM_LIMIT = 56 * 1024 * 1024


def _sigmoid(x):
    return jax.nn.sigmoid(x)


def _split3(x):
    hi = x.astype(BF16)
    r1 = x - hi.astype(F32)
    mid = r1.astype(BF16)
    lo = (r1 - mid.astype(F32)).astype(BF16)
    return hi, mid, lo


def _shift_rows(x, tail, sh):
    r = pltpu.roll(x, sh, 0)
    rt = pltpu.roll(tail, sh, 0)
    row = lax.broadcasted_iota(jnp.int32, tail.shape, 0)
    first = jnp.where(row < sh, rt, r[:SUBLANES])
    return jnp.concatenate([first, r[SUBLANES:]], axis=0)


def _conv_silu(xb, tail_sc, w_ref):
    x = xb.astype(F32)
    tail = tail_sc[...]
    w = w_ref[...]
    y = x * w[CONV_W - 1:CONV_W]
    for tap in range(CONV_W - 1):
        y = y + _shift_rows(x, tail, CONV_W - 1 - tap) * w[tap:tap + 1]
    tail_sc[...] = x[x.shape[0] - SUBLANES:]
    return y * _sigmoid(y)


def _mlstm_gates(r0, L, pm_sc, g_sc, bias_ref, cq_ref, ck_ref, tq_sc, tk_sc, wt_sc):
    nh, dh = N_HEADS_B, HEAD_DIM_B
    wb = nh * dh
    rows = slice(r0, r0 + L)
    q_all = _conv_silu(pm_sc[rows, 0:wb], tq_sc, cq_ref)
    k_all = _conv_silu(pm_sc[rows, wb:2 * wb], tk_sc, ck_ref) * (dh ** -0.5)

    g = g_sc[rows, :] + bias_ref[...]
    lf = jnp.minimum(g, 0.0) - jnp.log1p(jnp.exp(-jnp.abs(g)))
    rr = lax.broadcasted_iota(jnp.int32, (L, L), 0)
    cc = lax.broadcasted_iota(jnp.int32, (L, L), 1)
    causal = cc <= rr
    tri = jnp.where(causal, 1.0, 0.0).astype(BF16)
    cs = sum(jnp.dot(tri, part, preferred_element_type=F32) for part in _split3(lf))
    lane = lax.broadcasted_iota(jnp.int32, (L, LANES), 1)
    w = jnp.where(lane < nh, g, cs)
    wt_sc[...] = w.T
    return q_all, k_all, w, causal, lane


def _mlstm_head(h, r0, L, gates, pm_sc, nb_ref, y_ref, c_sc, n_sc, m_sc, wt_sc):
    nh, dh = N_HEADS_B, HEAD_DIM_B
    wb = nh * dh
    rows = slice(r0, r0 + L)
    hs = slice(h * dh, (h + 1) * dh)
    q_all, k_all, w, causal, lane = gates
    w_last = w[L - 1:L, :]
    lane1 = lax.broadcasted_iota(jnp.int32, (1, LANES), 1)

    def pick(x, ln, idx):
        return jnp.sum(jnp.where(ln == idx, x, 0.0), axis=1, keepdims=True)

    b_col = pick(w, lane, nh + h)
    i_col = pick(w, lane, h)
    b_row = wt_sc[nh + h:nh + h + 1, :]
    i_row = wt_sc[h:h + 1, :]
    b_last = pick(w_last, lane1, nh + h)
    m_prev = jnp.max(m_sc[h:h + 1, :], axis=1, keepdims=True)

    log_d = jnp.where(causal, b_col + (i_row - b_row), NEG)
    inter = b_col + m_prev
    m_row = jnp.maximum(inter, jnp.max(log_d, axis=1, keepdims=True))
    d_m = jnp.exp(log_d - m_row)
    w_inter = jnp.exp(inter - m_row)

    q = q_all[:, hs]
    k = k_all[:, hs]
    qb = q.astype(BF16)
    v = pm_sc[rows, 2 * wb + h * dh:2 * wb + (h + 1) * dh]
    s_m = lax.dot_general(qb, k.astype(BF16), (((1,), (1,)), ((), ())),
                          preferred_element_type=F32) * d_m
    c_old = c_sc[h]
    n_old = n_sc[h]
    num = (jnp.dot(s_m.astype(BF16), v, preferred_element_type=F32)
           + w_inter * jnp.dot(qb, c_old.astype(BF16), preferred_element_type=F32))
    den = (jnp.sum(s_m, axis=1, keepdims=True)
           + w_inter * jnp.sum(q * n_old, axis=1, keepdims=True))
    hval = num / jnp.maximum(jnp.abs(den), jnp.exp(-m_row))

    a_col = b_last - b_col + i_col
    m_new = jnp.maximum(b_last + m_prev, jnp.max(a_col, axis=0, keepdims=True))
    decay = jnp.exp(b_last + m_prev - m_new)
    kw = k * jnp.exp(a_col - m_new)
    c_sc[h] = decay * c_old + jnp.dot(kw.T.astype(BF16), v, preferred_element_type=F32)
    n_sc[h] = decay * n_old + jnp.sum(kw, axis=0, keepdims=True)
    m_sc[h:h + 1, :] = jnp.broadcast_to(m_new, (1, LANES))

    o = pm_sc[rows, 3 * wb + h * dh:3 * wb + (h + 1) * dh].astype(F32)
    hb = _sigmoid(o) * hval
    ms = jnp.mean(hb * hb, axis=-1, keepdims=True)
    yb = hb * lax.rsqrt(ms + EPS) * nb_ref[:, hs]
    z = pm_sc[rows, 4 * wb + h * dh:4 * wb + (h + 1) * dh].astype(F32)
    y_ref[rows, hs] = (yb * (z * _sigmoid(z))).astype(BF16)


def _proj_kernel(x_ref, g_ref, w9_ref, wgg_ref, wg_ref, wvt_ref, bias_ref, cq_ref, ck_ref, nb_ref,
                 pg_ref, pa_ref, vt_ref, y_ref, pm_sc, g_sc, c_sc, n_sc, m_sc, tq_sc, tk_sc, wt_sc,
                 *, tn, chunk, steps_per_seq, attn_chunks, mlstm_chunks):
    @pl.when(pl.program_id(0) % steps_per_seq == 0)
    def _():
        c_sc[...] = jnp.zeros(c_sc.shape, F32)
        n_sc[...] = jnp.zeros(n_sc.shape, F32)
        m_sc[...] = jnp.zeros(m_sc.shape, F32)
        tq_sc[...] = jnp.zeros(tq_sc.shape, F32)
        tk_sc[...] = jnp.zeros(tk_sc.shape, F32)

    x = x_ref[...]
    ms = jnp.mean(x * x, axis=-1, keepdims=True)
    h = (x * lax.rsqrt(ms + EPS) * g_ref[...]).astype(BF16)
    n_chunks = x.shape[0] // chunk
    n_gg = wgg_ref.shape[1] // tn

    def project(w_ref, src):
        return jnp.dot(h, w_ref[:, src * tn:(src + 1) * tn], preferred_element_type=F32).astype(BF16)

    def to_pm(c):
        pm_sc[:, c * tn:(c + 1) * tn] = project(w9_ref, mlstm_chunks[c])

    def to_p_gg(c):
        pg_ref[:, c * tn:(c + 1) * tn] = project(wgg_ref, c)

    def to_p_attn(c):
        res = project(w9_ref, attn_chunks[c])
        per = tn // LANES
        for hd in range(per):
            pa_ref[c * per + hd] = res[:, hd * LANES:(hd + 1) * LANES]

    def to_vt():
        vt_ref[...] = lax.dot_general(wvt_ref[...], h, (((1,), (1,)), ((), ())),
                                      preferred_element_type=F32).astype(BF16)

    fillers = ([functools.partial(to_p_gg, c) for c in range(n_gg)]
               + [functools.partial(to_p_attn, c) for c in range(len(attn_chunks))] + [to_vt])
    to_pm(0)
    to_pm(1)
    g_sc[...] = jnp.dot(h, wg_ref[...], preferred_element_type=F32)
    gates = []
    for ci in range(n_chunks):
        gates.append(_mlstm_gates(ci * chunk, chunk, pm_sc, g_sc, bias_ref, cq_ref, ck_ref,
                                  tq_sc, tk_sc, wt_sc.at[ci]))
        if 2 + ci < len(mlstm_chunks):
            to_pm(2 + ci)
    for c in range(2 + n_chunks, len(mlstm_chunks)):
        to_pm(c)
    for ci in range(n_chunks):
        for hd in range(N_HEADS_B):
            if fillers:
                fillers.pop(0)()
            _mlstm_head(hd, ci * chunk, chunk, gates[ci], pm_sc, nb_ref, y_ref,
                        c_sc, n_sc, m_sc, wt_sc.at[ci])
    for filler in fillers:
        filler()


def _proj_mlstm(x2, g, w9, w_gg, w_gate, w_vt, layer, bias_row, conv_q, conv_k, norm_b,
                *, seq, tm=512, tn=512, chunk=256):
    m, d = x2.shape
    n9 = w9.shape[2]
    ngg = w_gg.shape[2]
    wv = w_vt.shape[1]
    wb = N_HEADS_B * HEAD_DIM_B
    attn_chunks, mlstm_chunks = (0, 1, 3), (4, 5, 6, 7, 8)
    assert n9 == 9 * tn and wb == tn and wv == tn
    n_slabs = len(attn_chunks) * tn // LANES
    nt = seq // tm
    layer_blk = lambda i: (layer, 0, 0)
    const = lambda i: (0, 0)
    once = dict(pipeline_mode=pl.Buffered(1))
    return pl.pallas_call(
        functools.partial(_proj_kernel, tn=tn, chunk=chunk, steps_per_seq=nt,
                          attn_chunks=attn_chunks, mlstm_chunks=mlstm_chunks),
        grid=(m // tm,),
        in_specs=[
            pl.BlockSpec((tm, d), lambda i: (i, 0)),
            pl.BlockSpec((1, d), const),
            pl.BlockSpec((None, d, n9), layer_blk, **once),
            pl.BlockSpec((None, d, ngg), layer_blk, **once),
            pl.BlockSpec((None, d, LANES), layer_blk),
            pl.BlockSpec((None, wv, d), layer_blk),
            pl.BlockSpec((1, LANES), const),
            pl.BlockSpec((CONV_W, wb), const),
            pl.BlockSpec((CONV_W, wb), const),
            pl.BlockSpec((1, wb), const),
        ],
        out_specs=[
            pl.BlockSpec((tm, ngg), lambda i: (i, 0)),
            pl.BlockSpec((None, n_slabs, tm, LANES), lambda i: (i // nt, 0, i % nt, 0)),
            pl.BlockSpec((None, None, wv, tm), lambda i: (i // nt, i % nt, 0, 0)),
            pl.BlockSpec((tm, wb), lambda i: (i, 0)),
        ],
        out_shape=[
            jax.ShapeDtypeStruct((m, ngg), BF16),
            jax.ShapeDtypeStruct((m // seq, n_slabs, seq, LANES), BF16),
            jax.ShapeDtypeStruct((m // seq, nt, wv, tm), BF16),
            jax.ShapeDtypeStruct((m, wb), BF16),
        ],
        scratch_shapes=[
            pltpu.VMEM((tm, len(mlstm_chunks) * tn), BF16),
            pltpu.VMEM((tm, LANES), F32),
            pltpu.VMEM((N_HEADS_B, HEAD_DIM_B, HEAD_DIM_B), F32),
            pltpu.VMEM((N_HEADS_B, 1, HEAD_DIM_B), F32),
            pltpu.VMEM((SUBLANES, LANES), F32),
            pltpu.VMEM((SUBLANES, wb), F32),
            pltpu.VMEM((SUBLANES, wb), F32),
            pltpu.VMEM((tm // chunk, LANES, chunk), F32),
        ],
        compiler_params=pltpu.CompilerParams(
            dimension_semantics=("arbitrary",), vmem_limit_bytes=VMEM_LIMIT),
        name="projmlstm",
    )(x2, g, w9, w_gg, w_gate, w_vt, bias_row, conv_q, conv_k, norm_b)


def _attn_kernel(slopes_ref, qi_ref, kj_ref, lq_ref, q_ref, k_ref, vt_ref, za_ref, na_ref, o_ref,
                 qs_sc, s_sc, m_sc, acc_sc, *, tq, nq, lam_init):
    h = pl.program_id(1)
    d = HEAD_DIM_A
    dv = 2 * d
    slope2 = slopes_ref[h] * LOG2E
    rows2 = 2 * tq
    n_ct = rows2 // LANES
    n_steps = nq * (nq + 1) // 2

    lane = lax.broadcasted_iota(jnp.int32, (tq, LANES), 1)
    lane2 = lax.broadcasted_iota(jnp.int32, (rows2, LANES), 1)
    ones_cols = jnp.where(lane2 < 3, 1.0, 0.0).astype(BF16)
    for qb in range(nq):
        q = (q_ref[qb * tq:(qb + 1) * tq, :].astype(F32) * (d ** -0.5 * LOG2E)).astype(BF16)
        zero = jnp.zeros_like(q)
        qs = jnp.concatenate([jnp.where(lane < d, q, zero), jnp.where(lane >= d, q, zero)], axis=0)
        qs_sc[qb] = jnp.concatenate([qs, ones_cols], axis=1)
    kpos = lax.broadcasted_iota(jnp.int32, (tq, LANES), 0)
    b_hi, b_mid, b_lo = (part.astype(F32) for part in _split3(kpos.astype(F32) * slope2))
    k_bias = jnp.where(lane == 0, b_hi, jnp.where(lane == 1, b_mid, jnp.where(lane == 2, b_lo, 0.0)))
    k_bias = k_bias.astype(BF16)
    ones_row = jnp.where(lax.broadcasted_iota(jnp.int32, (2 * SUBLANES, tq), 0) == 0, 1.0, 0.0).astype(BF16)

    lq = lq_ref[...]
    lam = (jnp.exp(jnp.sum(lq[0:1] * lq[1:2], axis=1, keepdims=True))
           - jnp.exp(jnp.sum(lq[2:3] * lq[3:4], axis=1, keepdims=True)) + lam_init)

    def scores(t, slot):
        k = k_ref[pl.ds(pl.multiple_of(kj_ref[t] * tq, tq), tq), :]
        s = lax.dot_general(jnp.concatenate([k, k_bias], axis=1), qs_sc[qi_ref[t]],
                            (((1,), (1,)), ((), ())), preferred_element_type=F32)
        for c in range(n_ct):
            s_sc[slot, c] = s[:, c * LANES:(c + 1) * LANES]

    group = 2

    def value_product(j, ps):
        vt_aug = jnp.concatenate([vt_ref[j], ones_row], axis=0)
        return jnp.dot(vt_aug, jnp.concatenate(ps, axis=1), preferred_element_type=F32)

    def first_update(i, slot):
        for c0 in range(0, n_ct, group):
            ps = []
            for c in range(c0, c0 + group):
                cs = slice(c * LANES, (c + 1) * LANES)
                s = jnp.where(kpos <= lane + (c * LANES) % tq, s_sc[slot, c], NEG)
                m_new = jnp.max(s, axis=0, keepdims=True)
                ps.append(jnp.exp2(s - m_new).astype(BF16))
                m_sc[i, :, cs] = m_new
            acc_sc[i, :, c0 * LANES:(c0 + group) * LANES] = value_product(i, ps)

    def update(i, j, slot):
        off = ((j - i) * tq).astype(F32) * slope2
        for c0 in range(0, n_ct, group):
            gs = slice(c0 * LANES, (c0 + group) * LANES)
            ps, alphas = [], []
            for c in range(c0, c0 + group):
                cs = slice(c * LANES, (c + 1) * LANES)
                s = s_sc[slot, c]
                m_prev = m_sc[i, :, cs]
                m_new = jnp.maximum(m_prev, jnp.max(s, axis=0, keepdims=True) + off)
                alphas.append(jnp.exp2(m_prev - m_new))
                ps.append(jnp.exp2(s - (m_new - off)).astype(BF16))
                m_sc[i, :, cs] = m_new
            acc_sc[i, :, gs] = jnp.concatenate(alphas, axis=1) * acc_sc[i, :, gs] + value_product(j, ps)

    def finalize(i, carry):
        rows = pl.ds(pl.multiple_of(i * tq, tq), tq)
        ot = acc_sc[i, :dv, :] / acc_sc[i, dv:dv + 1, :]
        od = (ot[:, :tq] - lam * ot[:, tq:]).T
        ms = jnp.mean(od * od, axis=-1, keepdims=True)
        y = od * lax.rsqrt(ms + EPS) * na_ref[...] * (1.0 - lam_init)
        za = za_ref[rows, :].astype(F32)
        o_ref[rows, :] = (y * (za * _sigmoid(za))).astype(BF16)
        return carry

    unroll = s_sc.shape[0]
    assert nq % unroll == 0
    scores(0, 0)

    def diag_body(tt, carry):
        for u in range(unroll):
            t = unroll * tt + u
            scores(t + 1, (u + 1) % unroll)
            first_update(qi_ref[t], u)
        return carry

    lax.fori_loop(0, nq // unroll, diag_body, 0)

    n_trips = (n_steps - nq) // unroll
    assert (n_steps - nq) % unroll == 0 and n_trips == nq - 1 and unroll >= (nq - 1) / 2

    def off_body(tt, carry):
        for u in range(unroll):
            t = nq + unroll * tt + u
            scores(t + 1, (u + 1) % unroll)
            update(qi_ref[t], kj_ref[t], u)
            if u == 0:
                finalize(tt, carry)
        return carry

    lax.fori_loop(0, n_trips, off_body, 0)
    finalize(jnp.int32(nq - 1), 0)


def _attention(pa, vt4, slopes, lq, norm_a, *, lam_init, tq=512):
    b, _, s, _ = pa.shape
    h = N_HEADS_A
    nq = s // tq
    nt = vt4.shape[1]
    assert vt4.shape[3] == tq and nt == nq and nq % 2 == 0 and (nq * (nq - 1) // 2) % 2 == 0
    pairs = ([(i, i) for i in range(nq)] + [(i, j) for i in range(nq) for j in range(i)] + [(0, 0)])
    qi = jnp.asarray([p[0] for p in pairs], jnp.int32)
    kj = jnp.asarray([p[1] for p in pairs], jnp.int32)
    slab = lambda base: (lambda bi, hi: (bi, base + hi, 0, 0))
    smem = pl.BlockSpec(memory_space=pltpu.SMEM)
    return pl.pallas_call(
        functools.partial(_attn_kernel, tq=tq, nq=nq, lam_init=lam_init),
        grid=(b, h),
        in_specs=[
            smem, smem, smem,
            pl.BlockSpec((4, HEAD_DIM_A), lambda bi, hi: (0, 0)),
            pl.BlockSpec((None, None, s, LANES), slab(0)),
            pl.BlockSpec((None, None, s, LANES), slab(h)),
            pl.BlockSpec((None, nt, LANES, tq), lambda bi, hi: (bi, 0, hi, 0)),
            pl.BlockSpec((None, None, s, LANES), slab(2 * h)),
            pl.BlockSpec((1, LANES), lambda bi, hi: (0, hi)),
        ],
        out_specs=pl.BlockSpec((None, None, s, LANES), lambda bi, hi: (bi, hi, 0, 0)),
        out_shape=jax.ShapeDtypeStruct((b, h, s, LANES), BF16),
        scratch_shapes=[
            pltpu.VMEM((nq, 2 * tq, 2 * LANES), BF16),
            pltpu.VMEM((4, 2 * tq // LANES, tq, LANES), F32),
            pltpu.VMEM((nq, 1, 2 * tq), F32),
            pltpu.VMEM((nq, LANES + 2 * SUBLANES, 2 * tq), F32),
        ],
        compiler_params=pltpu.CompilerParams(
            dimension_semantics=("arbitrary", "arbitrary"), vmem_limit_bytes=VMEM_LIMIT),
        name="diffattn",
    )(slopes, qi, kj, lq, pa, pa, vt4, pa, norm_a)


def _merge_kernel(x_ref, ya_ref, yb_ref, ga_ref, gb_ref, wa_ref, wb_ref, wo_ref, g_ref, o_ref):
    ya = jnp.concatenate([ya_ref[hd] for hd in range(ya_ref.shape[0])], axis=1)
    a = jnp.dot(ya, wa_ref[...], preferred_element_type=F32)
    b = jnp.dot(yb_ref[...], wb_ref[...], preferred_element_type=F32)
    merged = _sigmoid(ga_ref[...].astype(F32)) * a + _sigmoid(gb_ref[...].astype(F32)) * b
    out = jnp.dot(merged.astype(BF16), wo_ref[...], preferred_element_type=F32)
    ms = jnp.mean(out * out, axis=-1, keepdims=True)
    o_ref[...] = x_ref[...] + out * lax.rsqrt(ms + EPS) * g_ref[...]


def _merge(x2, ya4, yb2, pg, w_a, w_b, w_out, layer, g, *, tm=512):
    m, d = x2.shape
    _, ha, seq, _ = ya4.shape
    wa = yb2.shape[1]
    nt = seq // tm
    row = lambda i: (i, 0)
    const = lambda i: (0, 0)
    return pl.pallas_call(
        _merge_kernel,
        grid=(m // tm,),
        in_specs=[
            pl.BlockSpec((tm, d), row),
            pl.BlockSpec((None, ha, tm, LANES), lambda i: (i // nt, 0, i % nt, 0)),
            pl.BlockSpec((tm, wa), row),
            pl.BlockSpec((tm, d), lambda i: (i, 0)),
            pl.BlockSpec((tm, d), lambda i: (i, 1)),
            pl.BlockSpec((None, wa, d), lambda i: (layer, 0, 0)),
            pl.BlockSpec((None, wa, d), lambda i: (layer, 0, 0)),
            pl.BlockSpec((None, d, d), lambda i: (layer, 0, 0)),
            pl.BlockSpec((1, d), const),
        ],
        out_specs=pl.BlockSpec((tm, d), row),
        out_shape=jax.ShapeDtypeStruct((m, d), F32),
        compiler_params=pltpu.CompilerParams(
            dimension_semantics=("arbitrary",), vmem_limit_bytes=VMEM_LIMIT),
        name="merge",
    )(x2, ya4, yb2, pg, pg, w_a, w_b, w_out, g)


def _wprep_main_kernel(wt_ref, w_ref, wvt_ref, *, va_chunk):
    w_ref[...] = wt_ref[...].T.astype(BF16)

    @pl.when(pl.program_id(1) == va_chunk)
    def _():
        wvt_ref[...] = wt_ref[...].astype(BF16)


def _wprep_tail_kernel(lo_ref, hi_ref, w_ref, gate_ref, *, ng):
    tn = w_ref.shape[1]
    both = jnp.concatenate([lo_ref[...], hi_ref[:SUBLANES, :]], axis=0)
    w_ref[...] = both[ng:ng + tn].T.astype(BF16)

    @pl.when(pl.program_id(1) == 0)
    def _():
        head = lo_ref[:LANES, :]
        row = lax.broadcasted_iota(jnp.int32, head.shape, 0)
        gate_ref[...] = jnp.where(row < ng, head, 0.0).T.astype(BF16)


def _prep_weights(w_in, n9, ng, va_start, *, tn=512):
    depth, d, n_in = w_in.shape
    ngg = n_in - n9 - ng
    assert n9 % tn == 0 and ngg % tn == 0 and ng == SUBLANES and va_start % tn == 0
    w_t = jnp.transpose(w_in, (0, 2, 1))
    params = pltpu.CompilerParams(dimension_semantics=("arbitrary", "arbitrary"),
                                  vmem_limit_bytes=VMEM_LIMIT)
    w9, w_vt = pl.pallas_call(
        functools.partial(_wprep_main_kernel, va_chunk=va_start // tn),
        grid=(depth, n9 // tn),
        in_specs=[pl.BlockSpec((None, tn, d), lambda l, c: (l, c, 0))],
        out_specs=[pl.BlockSpec((None, d, tn), lambda l, c: (l, 0, c)),
                   pl.BlockSpec((None, tn, d), lambda l, c: (l, 0, 0))],
        out_shape=[jax.ShapeDtypeStruct((depth, d, n9), BF16),
                   jax.ShapeDtypeStruct((depth, tn, d), BF16)],
        compiler_params=params, name="wprep_main",
    )(w_t)
    first = n9 // tn
    w_gg, w_gate = pl.pallas_call(
        functools.partial(_wprep_tail_kernel, ng=ng),
        grid=(depth, ngg // tn),
        in_specs=[pl.BlockSpec((None, tn, d), lambda l, c: (l, first + c, 0)),
                  pl.BlockSpec((None, tn, d), lambda l, c: (l, first + c + 1, 0))],
        out_specs=[pl.BlockSpec((None, d, tn), lambda l, c: (l, 0, c)),
                   pl.BlockSpec((None, d, LANES), lambda l, c: (l, 0, 0))],
        out_shape=[jax.ShapeDtypeStruct((depth, d, ngg), BF16),
                   jax.ShapeDtypeStruct((depth, d, LANES), BF16)],
        compiler_params=params, name="wprep_tail",
    )(w_t, w_t)
    return w9, w_gg, w_gate, w_vt


def kernel(x, norm_pre, norm_post, w_in, b_if, conv_qk, lambda_qk, norm_a, norm_b, w_a, w_b, w_out):
    bsz, seq, d = x.shape
    depth = w_in.shape[0]
    wa = N_HEADS_A * 2 * HEAD_DIM_A
    wb = N_HEADS_B * HEAD_DIM_B
    n9 = 4 * wa + 5 * wb
    ng = 2 * N_HEADS_B
    assert w_in.shape[2] == n9 + ng + 2 * d and d == 1024 and wa == 512 and wb == 512
    slopes = jnp.asarray(2.0 ** (-8.0 * np.arange(1, N_HEADS_A + 1) / N_HEADS_A), F32)
    w9, w_gg, w_gate, w_vt = _prep_weights(w_in, n9, ng, va_start=2 * wa)
    w_a_b, w_b_b, w_out_b = w_a.astype(BF16), w_b.astype(BF16), w_out.astype(BF16)

    x2 = x.reshape(bsz * seq, d)
    for l in range(depth):
        bias_row = jnp.pad(b_if[l], (0, LANES - ng)).reshape(1, LANES)
        lam_init = 0.8 - 0.6 * math.exp(-0.3 * l)
        pg, pa, vt4, yb2 = _proj_mlstm(x2, norm_pre[l].reshape(1, d), w9, w_gg, w_gate, w_vt, l, bias_row,
                                   conv_qk[l][:, :wb], conv_qk[l][:, wb:], norm_b[l].reshape(1, wb),
                                   seq=seq)
        ya = _attention(pa, vt4, slopes, lambda_qk[l], norm_a[l].reshape(1, wa), lam_init=lam_init)
        x2 = _merge(x2, ya, yb2, pg,
                    w_a_b, w_b_b, w_out_b, l, norm_post[l].reshape(1, d))
    return x2.reshape(bsz, seq, d)
```

```python
import functools
import math

import jax
import jax.numpy as jnp
import numpy as np
from jax import lax
from jax.experimental import pallas as pl
from jax.experimental.pallas import tpu as pltpu

F32 = jnp.float32
BF16 = jnp.bfloat16

N_HEADS_A = 4
HEAD_DIM_A = 64
N_HEADS_B = 4
HEAD_DIM_B = 128
CONV_W = 4
EPS = 1e-6

LANES = 128
SUBLANES = 8
NEG = -1e30
LOG2E = math.log2(math.e)
VMEM_LIMIT = 56 * 1024 * 1024


def _sigmoid(x):
    return jax.nn.sigmoid(x)


def _split3(x):
    hi = x.astype(BF16)
    r1 = x - hi.astype(F32)
    mid = r1.astype(BF16)
    lo = (r1 - mid.astype(F32)).astype(BF16)
    return hi, mid, lo


def _shift_rows(x, tail, sh):
    r = pltpu.roll(x, sh, 0)
    rt = pltpu.roll(tail, sh, 0)
    row = lax.broadcasted_iota(jnp.int32, tail.shape, 0)
    first = jnp.where(row < sh, rt, r[:SUBLANES])
    return jnp.concatenate([first, r[SUBLANES:]], axis=0)


def _conv_silu(xb, tail_sc, w_ref):
    x = xb.astype(F32)
    tail = tail_sc[...]
    w = w_ref[...]
    y = x * w[CONV_W - 1:CONV_W]
    for tap in range(CONV_W - 1):
        y = y + _shift_rows(x, tail, CONV_W - 1 - tap) * w[tap:tap + 1]
    tail_sc[...] = x[x.shape[0] - SUBLANES:]
    return y * _sigmoid(y)


def _mlstm_gates(r0, L, pm_sc, g_sc, bias_ref, cq_ref, ck_ref, tq_sc, tk_sc, wt_sc):
    nh, dh = N_HEADS_B, HEAD_DIM_B
    wb = nh * dh
    rows = slice(r0, r0 + L)
    q_all = _conv_silu(pm_sc[rows, 0:wb], tq_sc, cq_ref)
    k_all = _conv_silu(pm_sc[rows, wb:2 * wb], tk_sc, ck_ref) * (dh ** -0.5)

    g = g_sc[rows, :] + bias_ref[...]
    lf = jnp.minimum(g, 0.0) - jnp.log1p(jnp.exp(-jnp.abs(g)))
    rr = lax.broadcasted_iota(jnp.int32, (L, L), 0)
    cc = lax.broadcasted_iota(jnp.int32, (L, L), 1)
    causal = cc <= rr
    tri = jnp.where(causal, 1.0, 0.0).astype(BF16)
    cs = sum(jnp.dot(tri, part, preferred_element_type=F32) for part in _split3(lf))
    lane = lax.broadcasted_iota(jnp.int32, (L, LANES), 1)
    w = jnp.where(lane < nh, g, cs)
    wt_sc[...] = w.T
    return q_all, k_all, w, causal, lane


def _mlstm_head(h, r0, L, gates, pm_sc, nb_ref, y_ref, c_sc, n_sc, m_sc, wt_sc):
    nh, dh = N_HEADS_B, HEAD_DIM_B
    wb = nh * dh
    rows = slice(r0, r0 + L)
    hs = slice(h * dh, (h + 1) * dh)
    q_all, k_all, w, causal, lane = gates
    w_last = w[L - 1:L, :]
    lane1 = lax.broadcasted_iota(jnp.int32, (1, LANES), 1)

    def pick(x, ln, idx):
        return jnp.sum(jnp.where(ln == idx, x, 0.0), axis=1, keepdims=True)

    b_col = pick(w, lane, nh + h)
    i_col = pick(w, lane, h)
    b_row = wt_sc[nh + h:nh + h + 1, :]
    i_row = wt_sc[h:h + 1, :]
    b_last = pick(w_last, lane1, nh + h)
    m_prev = jnp.max(m_sc[h:h + 1, :], axis=1, keepdims=True)

    log_d = jnp.where(causal, b_col + (i_row - b_row), NEG)
    inter = b_col + m_prev
    m_row = jnp.maximum(inter, jnp.max(log_d, axis=1, keepdims=True))
    d_m = jnp.exp(log_d - m_row)
    w_inter = jnp.exp(inter - m_row)

    q = q_all[:, hs]
    k = k_all[:, hs]
    qb = q.astype(BF16)
    v = pm_sc[rows, 2 * wb + h * dh:2 * wb + (h + 1) * dh]
    s_m = lax.dot_general(qb, k.astype(BF16), (((1,), (1,)), ((), ())),
                          preferred_element_type=F32) * d_m
    c_old = c_sc[h]
    n_old = n_sc[h]
    num = (jnp.dot(s_m.astype(BF16), v, preferred_element_type=F32)
           + w_inter * jnp.dot(qb, c_old.astype(BF16), preferred_element_type=F32))
    den = (jnp.sum(s_m, axis=1, keepdims=True)
           + w_inter * jnp.sum(q * n_old, axis=1, keepdims=True))
    hval = num / jnp.maximum(jnp.abs(den), jnp.exp(-m_row))

    a_col = b_last - b_col + i_col
    m_new = jnp.maximum(b_last + m_prev, jnp.max(a_col, axis=0, keepdims=True))
    decay = jnp.exp(b_last + m_prev - m_new)
    kw = k * jnp.exp(a_col - m_new)
    c_sc[h] = decay * c_old + jnp.dot(kw.T.astype(BF16), v, preferred_element_type=F32)
    n_sc[h] = decay * n_old + jnp.sum(kw, axis=0, keepdims=True)
    m_sc[h:h + 1, :] = jnp.broadcast_to(m_new, (1, LANES))

    o = pm_sc[rows, 3 * wb + h * dh:3 * wb + (h + 1) * dh].astype(F32)
    hb = _sigmoid(o) * hval
    ms = jnp.mean(hb * hb, axis=-1, keepdims=True)
    yb = hb * lax.rsqrt(ms + EPS) * nb_ref[:, hs]
    z = pm_sc[rows, 4 * wb + h * dh:4 * wb + (h + 1) * dh].astype(F32)
    y_ref[rows, hs] = (yb * (z * _sigmoid(z))).astype(BF16)


def _proj_kernel(x_ref, g_ref, w9_ref, wgg_ref, wg_ref, wvt_ref, bias_ref, cq_ref, ck_ref, nb_ref,
                 pg_ref, pa_ref, vt_ref, y_ref, pm_sc, g_sc, c_sc, n_sc, m_sc, tq_sc, tk_sc, wt_sc,
                 *, tn, chunk, steps_per_seq, attn_chunks, mlstm_chunks):
    @pl.when(pl.program_id(0) % steps_per_seq == 0)
    def _():
        c_sc[...] = jnp.zeros(c_sc.shape, F32)
        n_sc[...] = jnp.zeros(n_sc.shape, F32)
        m_sc[...] = jnp.zeros(m_sc.shape, F32)
        tq_sc[...] = jnp.zeros(tq_sc.shape, F32)
        tk_sc[...] = jnp.zeros(tk_sc.shape, F32)

    x = x_ref[...]
    ms = jnp.mean(x * x, axis=-1, keepdims=True)
    h = (x * lax.rsqrt(ms + EPS) * g_ref[...]).astype(BF16)
    n_chunks = x.shape[0] // chunk
    n_gg = wgg_ref.shape[1] // tn

    def project(w_ref, src):
        return jnp.dot(h, w_ref[:, src * tn:(src + 1) * tn], preferred_element_type=F32).astype(BF16)

    def to_pm(c):
        pm_sc[:, c * tn:(c + 1) * tn] = project(w9_ref, mlstm_chunks[c])

    def to_p_gg(c):
        pg_ref[:, c * tn:(c + 1) * tn] = project(wgg_ref, c)

    def to_p_attn(c):
        res = project(w9_ref, attn_chunks[c])
        per = tn // LANES
        for hd in range(per):
            pa_ref[c * per + hd] = res[:, hd * LANES:(hd + 1) * LANES]

    def to_vt():
        vt_ref[...] = lax.dot_general(wvt_ref[...], h, (((1,), (1,)), ((), ())),
                                      preferred_element_type=F32).astype(BF16)

    fillers = ([functools.partial(to_p_gg, c) for c in range(n_gg)]
               + [functools.partial(to_p_attn, c) for c in range(len(attn_chunks))] + [to_vt])
    to_pm(0)
    to_pm(1)
    g_sc[...] = jnp.dot(h, wg_ref[...], preferred_element_type=F32)
    gates = []
    for ci in range(n_chunks):
        gates.append(_mlstm_gates(ci * chunk, chunk, pm_sc, g_sc, bias_ref, cq_ref, ck_ref,
                                  tq_sc, tk_sc, wt_sc.at[ci]))
        if 2 + ci < len(mlstm_chunks):
            to_pm(2 + ci)
    for c in range(2 + n_chunks, len(mlstm_chunks)):
        to_pm(c)
    for ci in range(n_chunks):
        for hd in range(N_HEADS_B):
            if fillers:
                fillers.pop(0)()
            _mlstm_head(hd, ci * chunk, chunk, gates[ci], pm_sc, nb_ref, y_ref,
                        c_sc, n_sc, m_sc, wt_sc.at[ci])
    for filler in fillers:
        filler()


def _proj_mlstm(x2, g, w9, w_gg, w_gate, w_vt, layer, bias_row, conv_q, conv_k, norm_b,
                *, seq, tm=512, tn=512, chunk=256):
    m, d = x2.shape
    n9 = w9.shape[2]
    ngg = w_gg.shape[2]
    wv = w_vt.shape[1]
    wb = N_HEADS_B * HEAD_DIM_B
    attn_chunks, mlstm_chunks = (0, 1, 3), (4, 5, 6, 7, 8)
    assert n9 == 9 * tn and wb == tn and wv == tn
    n_slabs = len(attn_chunks) * tn // LANES
    nt = seq // tm
    layer_blk = lambda i: (layer, 0, 0)
    const = lambda i: (0, 0)
    once = dict(pipeline_mode=pl.Buffered(1))
    return pl.pallas_call(
        functools.partial(_proj_kernel, tn=tn, chunk=chunk, steps_per_seq=nt,
                          attn_chunks=attn_chunks, mlstm_chunks=mlstm_chunks),
        grid=(m // tm,),
        in_specs=[
            pl.BlockSpec((tm, d), lambda i: (i, 0)),
            pl.BlockSpec((1, d), const),
            pl.BlockSpec((None, d, n9), layer_blk, **once),
            pl.BlockSpec((None, d, ngg), layer_blk, **once),
            pl.BlockSpec((None, d, LANES), layer_blk),
            pl.BlockSpec((None, wv, d), layer_blk),
            pl.BlockSpec((1, LANES), const),
            pl.BlockSpec((CONV_W, wb), const),
            pl.BlockSpec((CONV_W, wb), const),
            pl.BlockSpec((1, wb), const),
        ],
        out_specs=[
            pl.BlockSpec((tm, ngg), lambda i: (i, 0)),
            pl.BlockSpec((None, n_slabs, tm, LANES), lambda i: (i // nt, 0, i % nt, 0)),
            pl.BlockSpec((None, None, wv, tm), lambda i: (i // nt, i % nt, 0, 0)),
            pl.BlockSpec((tm, wb), lambda i: (i, 0)),
        ],
        out_shape=[
            jax.ShapeDtypeStruct((m, ngg), BF16),
            jax.ShapeDtypeStruct((m // seq, n_slabs, seq, LANES), BF16),
            jax.ShapeDtypeStruct((m // seq, nt, wv, tm), BF16),
            jax.ShapeDtypeStruct((m, wb), BF16),
        ],
        scratch_shapes=[
            pltpu.VMEM((tm, len(mlstm_chunks) * tn), BF16),
            pltpu.VMEM((tm, LANES), F32),
            pltpu.VMEM((N_HEADS_B, HEAD_DIM_B, HEAD_DIM_B), F32),
            pltpu.VMEM((N_HEADS_B, 1, HEAD_DIM_B), F32),
            pltpu.VMEM((SUBLANES, LANES), F32),
            pltpu.VMEM((SUBLANES, wb), F32),
            pltpu.VMEM((SUBLANES, wb), F32),
            pltpu.VMEM((tm // chunk, LANES, chunk), F32),
        ],
        compiler_params=pltpu.CompilerParams(
            dimension_semantics=("arbitrary",), vmem_limit_bytes=VMEM_LIMIT),
        name="projmlstm",
    )(x2, g, w9, w_gg, w_gate, w_vt, bias_row, conv_q, conv_k, norm_b)


def _attn_kernel(slopes_ref, qi_ref, kj_ref, lq_ref, q_ref, k_ref, vt_ref, za_ref, na_ref, o_ref,
                 qs_sc, s_sc, m_sc, acc_sc, *, tq, nq, lam_init):
    h = pl.program_id(1)
    d = HEAD_DIM_A
    dv = 2 * d
    slope2 = slopes_ref[h] * LOG2E
    rows2 = 2 * tq
    n_ct = rows2 // LANES
    n_steps = nq * (nq + 1) // 2

    lane = lax.broadcasted_iota(jnp.int32, (tq, LANES), 1)
    lane2 = lax.broadcasted_iota(jnp.int32, (rows2, LANES), 1)
    ones_cols = jnp.where(lane2 < 3, 1.0, 0.0).astype(BF16)
    for qb in range(nq):
        q = (q_ref[qb * tq:(qb + 1) * tq, :].astype(F32) * (d ** -0.5 * LOG2E)).astype(BF16)
        zero = jnp.zeros_like(q)
        qs = jnp.concatenate([jnp.where(lane < d, q, zero), jnp.where(lane >= d, q, zero)], axis=0)
        qs_sc[qb] = jnp.concatenate([qs, ones_cols], axis=1)
    kpos = lax.broadcasted_iota(jnp.int32, (tq, LANES), 0)
    b_hi, b_mid, b_lo = (part.astype(F32) for part in _split3(kpos.astype(F32) * slope2))
    k_bias = jnp.where(lane == 0, b_hi, jnp.where(lane == 1, b_mid, jnp.where(lane == 2, b_lo, 0.0)))
    k_bias = k_bias.astype(BF16)
    ones_row = jnp.where(lax.broadcasted_iota(jnp.int32, (2 * SUBLANES, tq), 0) == 0, 1.0, 0.0).astype(BF16)

    lq = lq_ref[...]
    lam = (jnp.exp(jnp.sum(lq[0:1] * lq[1:2], axis=1, keepdims=True))
           - jnp.exp(jnp.sum(lq[2:3] * lq[3:4], axis=1, keepdims=True)) + lam_init)

    group = 2
    n_groups = n_ct // group

    def scores(t, slot, parts=None):
        k = k_ref[pl.ds(pl.multiple_of(kj_ref[t] * tq, tq), tq), :]
        k_aug = jnp.concatenate([k, k_bias], axis=1)
        for g in (range(n_groups) if parts is None else parts):
            rows = slice(g * group * LANES, (g + 1) * group * LANES)
            s = lax.dot_general(k_aug, qs_sc[qi_ref[t], rows, :], (((1,), (1,)), ((), ())),
                                preferred_element_type=F32)
            for c in range(group):
                s_sc[slot, g * group + c] = s[:, c * LANES:(c + 1) * LANES]

    def value_product(j, ps):
        vt_aug = jnp.concatenate([vt_ref[j], ones_row], axis=0)
        return jnp.dot(vt_aug, jnp.concatenate(ps, axis=1), preferred_element_type=F32)

    def first_update(i, slot, between=None):
        for c0 in range(0, n_ct, group):
            ps = []
            for c in range(c0, c0 + group):
                cs = slice(c * LANES, (c + 1) * LANES)
                s = jnp.where(kpos <= lane + (c * LANES) % tq, s_sc[slot, c], NEG)
                m_new = jnp.max(s, axis=0, keepdims=True)
                ps.append(jnp.exp2(s - m_new).astype(BF16))
                m_sc[i, :, cs] = m_new
            acc_sc[i, :, c0 * LANES:(c0 + group) * LANES] = value_product(i, ps)
            if between is not None:
                between(c0 // group)

    def update(i, j, slot, between=None):
        off = ((j - i) * tq).astype(F32) * slope2
        for c0 in range(0, n_ct, group):
            gs = slice(c0 * LANES, (c0 + group) * LANES)
            ps, alphas = [], []
            for c in range(c0, c0 + group):
                cs = slice(c * LANES, (c + 1) * LANES)
                s = s_sc[slot, c]
                m_prev = m_sc[i, :, cs]
                m_new = jnp.maximum(m_prev, jnp.max(s, axis=0, keepdims=True) + off)
                alphas.append(jnp.exp2(m_prev - m_new))
                ps.append(jnp.exp2(s - (m_new - off)).astype(BF16))
                m_sc[i, :, cs] = m_new
            acc_sc[i, :, gs] = jnp.concatenate(alphas, axis=1) * acc_sc[i, :, gs] + value_product(j, ps)
            if between is not None:
                between(c0 // group)

    def finalize(i, carry):
        rows = pl.ds(pl.multiple_of(i * tq, tq), tq)
        ot = acc_sc[i, :dv, :] / acc_sc[i, dv:dv + 1, :]
        od = (ot[:, :tq] - lam * ot[:, tq:]).T
        ms = jnp.mean(od * od, axis=-1, keepdims=True)
        y = od * lax.rsqrt(ms + EPS) * na_ref[...] * (1.0 - lam_init)
        za = za_ref[rows, :].astype(F32)
        o_ref[rows, :] = (y * (za * _sigmoid(za))).astype(BF16)
        return carry

    unroll = s_sc.shape[0]
    assert nq % unroll == 0
    scores(0, 0)

    def next_scores(t, u):
        def emit(g):
            if g == 0:
                scores(t + 1, (u + 1) % unroll)
        return emit

    def diag_body(tt, carry):
        for u in range(unroll):
            t = unroll * tt + u
            first_update(qi_ref[t], u, between=next_scores(t, u))
        return carry

    lax.fori_loop(0, nq // unroll, diag_body, 0)

    n_trips = (n_steps - nq) // unroll
    assert (n_steps - nq) % unroll == 0 and n_trips == nq - 1 and unroll >= (nq - 1) / 2

    def off_body(tt, carry):
        for u in range(unroll):
            t = nq + unroll * tt + u
            update(qi_ref[t], kj_ref[t], u, between=next_scores(t, u))
            if u == 0:
                finalize(tt, carry)
        return carry

    lax.fori_loop(0, n_trips, off_body, 0)
    finalize(jnp.int32(nq - 1), 0)


def _attention(pa, vt4, slopes, lq, norm_a, *, lam_init, tq=512):
    b, _, s, _ = pa.shape
    h = N_HEADS_A
    nq = s // tq
    nt = vt4.shape[1]
    assert vt4.shape[3] == tq and nt == nq and nq % 2 == 0 and (nq * (nq - 1) // 2) % 2 == 0
    pairs = ([(i, i) for i in range(nq)] + [(i, j) for i in range(nq) for j in range(i)] + [(0, 0)])
    qi = jnp.asarray([p[0] for p in pairs], jnp.int32)
    kj = jnp.asarray([p[1] for p in pairs], jnp.int32)
    slab = lambda base: (lambda bi, hi: (bi, base + hi, 0, 0))
    smem = pl.BlockSpec(memory_space=pltpu.SMEM)
    return pl.pallas_call(
        functools.partial(_attn_kernel, tq=tq, nq=nq, lam_init=lam_init),
        grid=(b, h),
        in_specs=[
            smem, smem, smem,
            pl.BlockSpec((4, HEAD_DIM_A), lambda bi, hi: (0, 0)),
            pl.BlockSpec((None, None, s, LANES), slab(0)),
            pl.BlockSpec((None, None, s, LANES), slab(h)),
            pl.BlockSpec((None, nt, LANES, tq), lambda bi, hi: (bi, 0, hi, 0)),
            pl.BlockSpec((None, None, s, LANES), slab(2 * h)),
            pl.BlockSpec((1, LANES), lambda bi, hi: (0, hi)),
        ],
        out_specs=pl.BlockSpec((None, None, s, LANES), lambda bi, hi: (bi, hi, 0, 0)),
        out_shape=jax.ShapeDtypeStruct((b, h, s, LANES), BF16),
        scratch_shapes=[
            pltpu.VMEM((nq, 2 * tq, 2 * LANES), BF16),
            pltpu.VMEM((4, 2 * tq // LANES, tq, LANES), F32),
            pltpu.VMEM((nq, 1, 2 * tq), F32),
            pltpu.VMEM((nq, LANES + 2 * SUBLANES, 2 * tq), F32),
        ],
        compiler_params=pltpu.CompilerParams(
            dimension_semantics=("arbitrary", "arbitrary"), vmem_limit_bytes=VMEM_LIMIT),
        name="diffattn",
    )(slopes, qi, kj, lq, pa, pa, vt4, pa, norm_a)


def _merge_kernel(x_ref, ya_ref, yb_ref, ga_ref, gb_ref, wa_ref, wb_ref, wo_ref, g_ref, o_ref):
    ya = jnp.concatenate([ya_ref[hd] for hd in range(ya_ref.shape[0])], axis=1)
    a = jnp.dot(ya, wa_ref[...], preferred_element_type=F32)
    b = jnp.dot(yb_ref[...], wb_ref[...], preferred_element_type=F32)
    merged = _sigmoid(ga_ref[...].astype(F32)) * a + _sigmoid(gb_ref[...].astype(F32)) * b
    out = jnp.dot(merged.astype(BF16), wo_ref[...], preferred_element_type=F32)
    ms = jnp.mean(out * out, axis=-1, keepdims=True)
    o_ref[...] = x_ref[...] + out * lax.rsqrt(ms + EPS) * g_ref[...]


def _merge(x2, ya4, yb2, pg, w_a, w_b, w_out, layer, g, *, tm=1024):
    m, d = x2.shape
    _, ha, seq, _ = ya4.shape
    wa = yb2.shape[1]
    nt = seq // tm
    row = lambda i: (i, 0)
    const = lambda i: (0, 0)
    return pl.pallas_call(
        _merge_kernel,
        grid=(m // tm,),
        in_specs=[
            pl.BlockSpec((tm, d), row),
            pl.BlockSpec((None, ha, tm, LANES), lambda i: (i // nt, 0, i % nt, 0)),
            pl.BlockSpec((tm, wa), row),
            pl.BlockSpec((tm, d), lambda i: (i, 0)),
            pl.BlockSpec((tm, d), lambda i: (i, 1)),
            pl.BlockSpec((None, wa, d), lambda i: (layer, 0, 0)),
            pl.BlockSpec((None, wa, d), lambda i: (layer, 0, 0)),
            pl.BlockSpec((None, d, d), lambda i: (layer, 0, 0)),
            pl.BlockSpec((1, d), const),
        ],
        out_specs=pl.BlockSpec((tm, d), row),
        out_shape=jax.ShapeDtypeStruct((m, d), F32),
        compiler_params=pltpu.CompilerParams(
            dimension_semantics=("arbitrary",), vmem_limit_bytes=VMEM_LIMIT),
        name="merge",
    )(x2, ya4, yb2, pg, pg, w_a, w_b, w_out, g)


def _wprep_main_kernel(wt_ref, w_ref, wvt_ref, *, va_chunk):
    w_ref[...] = wt_ref[...].T.astype(BF16)

    @pl.when(pl.program_id(1) == va_chunk)
    def _():
        wvt_ref[...] = wt_ref[...].astype(BF16)


def _wprep_tail_kernel(lo_ref, hi_ref, w_ref, gate_ref, *, ng):
    tn = w_ref.shape[1]
    both = jnp.concatenate([lo_ref[...], hi_ref[:SUBLANES, :]], axis=0)
    w_ref[...] = both[ng:ng + tn].T.astype(BF16)

    @pl.when(pl.program_id(1) == 0)
    def _():
        head = lo_ref[:LANES, :]
        row = lax.broadcasted_iota(jnp.int32, head.shape, 0)
        gate_ref[...] = jnp.where(row < ng, head, 0.0).T.astype(BF16)


def _prep_weights(w_in, n9, ng, va_start, *, tn=512):
    depth, d, n_in = w_in.shape
    ngg = n_in - n9 - ng
    assert n9 % tn == 0 and ngg % tn == 0 and ng == SUBLANES and va_start % tn == 0
    w_t = jnp.transpose(w_in, (0, 2, 1))
    params = pltpu.CompilerParams(dimension_semantics=("arbitrary", "arbitrary"),
                                  vmem_limit_bytes=VMEM_LIMIT)
    w9, w_vt = pl.pallas_call(
        functools.partial(_wprep_main_kernel, va_chunk=va_start // tn),
        grid=(depth, n9 // tn),
        in_specs=[pl.BlockSpec((None, tn, d), lambda l, c: (l, c, 0))],
        out_specs=[pl.BlockSpec((None, d, tn), lambda l, c: (l, 0, c)),
                   pl.BlockSpec((None, tn, d), lambda l, c: (l, 0, 0))],
        out_shape=[jax.ShapeDtypeStruct((depth, d, n9), BF16),
                   jax.ShapeDtypeStruct((depth, tn, d), BF16)],
        compiler_params=params, name="wprep_main",
    )(w_t)
    first = n9 // tn
    w_gg, w_gate = pl.pallas_call(
        functools.partial(_wprep_tail_kernel, ng=ng),
        grid=(depth, ngg // tn),
        in_specs=[pl.BlockSpec((None, tn, d), lambda l, c: (l, first + c, 0)),
                  pl.BlockSpec((None, tn, d), lambda l, c: (l, first + c + 1, 0))],
        out_specs=[pl.BlockSpec((None, d, tn), lambda l, c: (l, 0, c)),
                   pl.BlockSpec((None, d, LANES), lambda l, c: (l, 0, 0))],
        out_shape=[jax.ShapeDtypeStruct((depth, d, ngg), BF16),
                   jax.ShapeDtypeStruct((depth, d, LANES), BF16)],
        compiler_params=params, name="wprep_tail",
    )(w_t, w_t)
    return w9, w_gg, w_gate, w_vt


def kernel(x, norm_pre, norm_post, w_in, b_if, conv_qk, lambda_qk, norm_a, norm_b, w_a, w_b, w_out):
    bsz, seq, d = x.shape
    depth = w_in.shape[0]
    wa = N_HEADS_A * 2 * HEAD_DIM_A
    wb = N_HEADS_B * HEAD_DIM_B
    n9 = 4 * wa + 5 * wb
    ng = 2 * N_HEADS_B
    assert w_in.shape[2] == n9 + ng + 2 * d and d == 1024 and wa == 512 and wb == 512
    slopes = jnp.asarray(2.0 ** (-8.0 * np.arange(1, N_HEADS_A + 1) / N_HEADS_A), F32)
    w9, w_gg, w_gate, w_vt = _prep_weights(w_in, n9, ng, va_start=2 * wa)
    w_a_b, w_b_b, w_out_b = w_a.astype(BF16), w_b.astype(BF16), w_out.astype(BF16)

    x2 = x.reshape(bsz * seq, d)
    for l in range(depth):
        bias_row = jnp.pad(b_if[l], (0, LANES - ng)).reshape(1, LANES)
        lam_init = 0.8 - 0.6 * math.exp(-0.3 * l)
        pg, pa, vt4, yb2 = _proj_mlstm(x2, norm_pre[l].reshape(1, d), w9, w_gg, w_gate, w_vt, l, bias_row,
                                   conv_qk[l][:, :wb], conv_qk[l][:, wb:], norm_b[l].reshape(1, wb),
                                   seq=seq)
        ya = _attention(pa, vt4, slopes, lambda_qk[l], norm_a[l].reshape(1, wa), lam_init=lam_init)
        x2 = _merge(x2, ya, yb2, pg,
                    w_a_b, w_b_b, w_out_b, l, norm_post[l].reshape(1, d))
    return x2.reshape(bsz, seq, d)
```

```python
import functools
import math

import jax
import jax.numpy as jnp
import numpy as np
from jax import lax
from jax.experimental import pallas as pl
from jax.experimental.pallas import tpu as pltpu

F32 = jnp.float32
BF16 = jnp.bfloat16

N_HEADS_A = 4
HEAD_DIM_A = 64
N_HEADS_B = 4
HEAD_DIM_B = 128
CONV_W = 4
EPS = 1e-6

LANES = 128
SUBLANES = 8
NEG = -1e30
LOG2E = math.log2(math.e)
VMEM_LIMIT = 56 * 1024 * 1024


def _sigmoid(x):
    return jax.nn.sigmoid(x)


def _split3(x):
    hi = x.astype(BF16)
    r1 = x - hi.astype(F32)
    mid = r1.astype(BF16)
    lo = (r1 - mid.astype(F32)).astype(BF16)
    return hi, mid, lo


def _shift_rows(x, tail, sh):
    r = pltpu.roll(x, sh, 0)
    rt = pltpu.roll(tail, sh, 0)
    row = lax.broadcasted_iota(jnp.int32, tail.shape, 0)
    first = jnp.where(row < sh, rt, r[:SUBLANES])
    return jnp.concatenate([first, r[SUBLANES:]], axis=0)


def _conv_silu(xb, tail_sc, w_ref):
    x = xb.astype(F32)
    tail = tail_sc[...]
    w = w_ref[...]
    y = x * w[CONV_W - 1:CONV_W]
    for tap in range(CONV_W - 1):
        y = y + _shift_rows(x, tail, CONV_W - 1 - tap) * w[tap:tap + 1]
    tail_sc[...] = x[x.shape[0] - SUBLANES:]
    return y * _sigmoid(y)


def _mlstm_gates(r0, L, pm_sc, g_sc, bias_ref, cq_ref, ck_ref, tq_sc, tk_sc, wt_sc):
    nh, dh = N_HEADS_B, HEAD_DIM_B
    wb = nh * dh
    rows = slice(r0, r0 + L)
    q_all = _conv_silu(pm_sc[rows, 0:wb], tq_sc, cq_ref)
    k_all = _conv_silu(pm_sc[rows, wb:2 * wb], tk_sc, ck_ref) * (dh ** -0.5)

    g = g_sc[rows, :] + bias_ref[...]
    lf = jnp.minimum(g, 0.0) - jnp.log1p(jnp.exp(-jnp.abs(g)))
    rr = lax.broadcasted_iota(jnp.int32, (L, L), 0)
    cc = lax.broadcasted_iota(jnp.int32, (L, L), 1)
    causal = cc <= rr
    tri = jnp.where(causal, 1.0, 0.0).astype(BF16)
    cs = sum(jnp.dot(tri, part, preferred_element_type=F32) for part in _split3(lf))
    lane = lax.broadcasted_iota(jnp.int32, (L, LANES), 1)
    w = jnp.where(lane < nh, g, cs)
    wt_sc[...] = w.T
    return q_all, k_all, w, causal, lane


def _mlstm_head(h, r0, L, gates, pm_sc, nb_ref, y_ref, c_sc, n_sc, m_sc, wt_sc):
    nh, dh = N_HEADS_B, HEAD_DIM_B
    wb = nh * dh
    rows = slice(r0, r0 + L)
    hs = slice(h * dh, (h + 1) * dh)
    q_all, k_all, w, causal, lane = gates
    w_last = w[L - 1:L, :]
    lane1 = lax.broadcasted_iota(jnp.int32, (1, LANES), 1)

    def pick(x, ln, idx):
        return jnp.sum(jnp.where(ln == idx, x, 0.0), axis=1, keepdims=True)

    b_col = pick(w, lane, nh + h)
    i_col = pick(w, lane, h)
    b_row = wt_sc[nh + h:nh + h + 1, :]
    i_row = wt_sc[h:h + 1, :]
    b_last = pick(w_last, lane1, nh + h)
    m_prev = jnp.max(m_sc[h:h + 1, :], axis=1, keepdims=True)

    log_d = jnp.where(causal, b_col + (i_row - b_row), NEG)
    inter = b_col + m_prev
    m_row = jnp.maximum(inter, jnp.max(log_d, axis=1, keepdims=True))
    d_m = jnp.exp(log_d - m_row)
    w_inter = jnp.exp(inter - m_row)

    q = q_all[:, hs]
    k = k_all[:, hs]
    qb = q.astype(BF16)
    v = pm_sc[rows, 2 * wb + h * dh:2 * wb + (h + 1) * dh]
    s_m = lax.dot_general(qb, k.astype(BF16), (((1,), (1,)), ((), ())),
                          preferred_element_type=F32) * d_m
    c_old = c_sc[h]
    n_old = n_sc[h]
    num = (jnp.dot(s_m.astype(BF16), v, preferred_element_type=F32)
           + w_inter * jnp.dot(qb, c_old.astype(BF16), preferred_element_type=F32))
    den = (jnp.sum(s_m, axis=1, keepdims=True)
           + w_inter * jnp.sum(q * n_old, axis=1, keepdims=True))
    hval = num / jnp.maximum(jnp.abs(den), jnp.exp(-m_row))

    a_col = b_last - b_col + i_col
    m_new = jnp.maximum(b_last + m_prev, jnp.max(a_col, axis=0, keepdims=True))
    decay = jnp.exp(b_last + m_prev - m_new)
    kw = k * jnp.exp(a_col - m_new)
    c_sc[h] = decay * c_old + jnp.dot(kw.T.astype(BF16), v, preferred_element_type=F32)
    n_sc[h] = decay * n_old + jnp.sum(kw, axis=0, keepdims=True)
    m_sc[h:h + 1, :] = jnp.broadcast_to(m_new, (1, LANES))

    o = pm_sc[rows, 3 * wb + h * dh:3 * wb + (h + 1) * dh].astype(F32)
    hb = _sigmoid(o) * hval
    ms = jnp.mean(hb * hb, axis=-1, keepdims=True)
    yb = hb * lax.rsqrt(ms + EPS) * nb_ref[:, hs]
    z = pm_sc[rows, 4 * wb + h * dh:4 * wb + (h + 1) * dh].astype(F32)
    y_ref[rows, hs] = (yb * (z * _sigmoid(z))).astype(BF16)


def _proj_kernel(x_ref, g_ref, w9_ref, wgg_ref, wg_ref, wvt_ref, bias_ref, cq_ref, ck_ref, nb_ref,
                 pg_ref, pa_ref, vt_ref, y_ref, pm_sc, g_sc, c_sc, n_sc, m_sc, tq_sc, tk_sc, wt_sc,
                 *, tn, chunk, steps_per_seq, attn_chunks, mlstm_chunks):
    @pl.when(pl.program_id(0) % steps_per_seq == 0)
    def _():
        c_sc[...] = jnp.zeros(c_sc.shape, F32)
        n_sc[...] = jnp.zeros(n_sc.shape, F32)
        m_sc[...] = jnp.zeros(m_sc.shape, F32)
        tq_sc[...] = jnp.zeros(tq_sc.shape, F32)
        tk_sc[...] = jnp.zeros(tk_sc.shape, F32)

    x = x_ref[...]
    ms = jnp.mean(x * x, axis=-1, keepdims=True)
    h = (x * lax.rsqrt(ms + EPS) * g_ref[...]).astype(BF16)
    n_chunks = x.shape[0] // chunk
    n_gg = wgg_ref.shape[1] // tn

    def project(w_ref, src):
        return jnp.dot(h, w_ref[:, src * tn:(src + 1) * tn], preferred_element_type=F32).astype(BF16)

    def to_pm(c):
        pm_sc[:, c * tn:(c + 1) * tn] = project(w9_ref, mlstm_chunks[c])

    def to_p_gg(c):
        pg_ref[:, c * tn:(c + 1) * tn] = project(wgg_ref, c)

    def to_p_attn(c):
        res = project(w9_ref, attn_chunks[c])
        per = tn // LANES
        for hd in range(per):
            pa_ref[c * per + hd] = res[:, hd * LANES:(hd + 1) * LANES]

    def to_vt():
        vt_ref[...] = lax.dot_general(wvt_ref[...], h, (((1,), (1,)), ((), ())),
                                      preferred_element_type=F32).astype(BF16)

    fillers = ([functools.partial(to_p_gg, c) for c in range(n_gg)]
               + [functools.partial(to_p_attn, c) for c in range(len(attn_chunks))] + [to_vt])
    to_pm(0)
    to_pm(1)
    g_sc[...] = jnp.dot(h, wg_ref[...], preferred_element_type=F32)
    gates = []
    for ci in range(n_chunks):
        gates.append(_mlstm_gates(ci * chunk, chunk, pm_sc, g_sc, bias_ref, cq_ref, ck_ref,
                                  tq_sc, tk_sc, wt_sc.at[ci]))
        if 2 + ci < len(mlstm_chunks):
            to_pm(2 + ci)
    for c in range(2 + n_chunks, len(mlstm_chunks)):
        to_pm(c)
    for ci in range(n_chunks):
        for hd in range(N_HEADS_B):
            if fillers:
                fillers.pop(0)()
            _mlstm_head(hd, ci * chunk, chunk, gates[ci], pm_sc, nb_ref, y_ref,
                        c_sc, n_sc, m_sc, wt_sc.at[ci])
    for filler in fillers:
        filler()


def _proj_mlstm(x2, g, w9, w_gg, w_gate, w_vt, layer, bias_row, conv_q, conv_k, norm_b,
                *, seq, tm=512, tn=512, chunk=256):
    m, d = x2.shape
    n9 = w9.shape[2]
    ngg = w_gg.shape[2]
    wv = w_vt.shape[1]
    wb = N_HEADS_B * HEAD_DIM_B
    attn_chunks, mlstm_chunks = (0, 1, 3), (4, 5, 6, 7, 8)
    assert n9 == 9 * tn and wb == tn and wv == tn
    n_slabs = len(attn_chunks) * tn // LANES
    nt = seq // tm
    layer_blk = lambda i: (layer, 0, 0)
    const = lambda i: (0, 0)
    once = dict(pipeline_mode=pl.Buffered(1))
    return pl.pallas_call(
        functools.partial(_proj_kernel, tn=tn, chunk=chunk, steps_per_seq=nt,
                          attn_chunks=attn_chunks, mlstm_chunks=mlstm_chunks),
        grid=(m // tm,),
        in_specs=[
            pl.BlockSpec((tm, d), lambda i: (i, 0)),
            pl.BlockSpec((1, d), const),
            pl.BlockSpec((None, d, n9), layer_blk, **once),
            pl.BlockSpec((None, d, ngg), layer_blk, **once),
            pl.BlockSpec((None, d, LANES), layer_blk),
            pl.BlockSpec((None, wv, d), layer_blk),
            pl.BlockSpec((1, LANES), const),
            pl.BlockSpec((CONV_W, wb), const),
            pl.BlockSpec((CONV_W, wb), const),
            pl.BlockSpec((1, wb), const),
        ],
        out_specs=[
            pl.BlockSpec((tm, ngg), lambda i: (i, 0)),
            pl.BlockSpec((None, n_slabs, tm, LANES), lambda i: (i // nt, 0, i % nt, 0)),
            pl.BlockSpec((None, None, wv, tm), lambda i: (i // nt, i % nt, 0, 0)),
            pl.BlockSpec((tm, wb), lambda i: (i, 0)),
        ],
        out_shape=[
            jax.ShapeDtypeStruct((m, ngg), BF16),
            jax.ShapeDtypeStruct((m // seq, n_slabs, seq, LANES), BF16),
            jax.ShapeDtypeStruct((m // seq, nt, wv, tm), BF16),
            jax.ShapeDtypeStruct((m, wb), BF16),
        ],
        scratch_shapes=[
            pltpu.VMEM((tm, len(mlstm_chunks) * tn), BF16),
            pltpu.VMEM((tm, LANES), F32),
            pltpu.VMEM((N_HEADS_B, HEAD_DIM_B, HEAD_DIM_B), F32),
            pltpu.VMEM((N_HEADS_B, 1, HEAD_DIM_B), F32),
            pltpu.VMEM((SUBLANES, LANES), F32),
            pltpu.VMEM((SUBLANES, wb), F32),
            pltpu.VMEM((SUBLANES, wb), F32),
            pltpu.VMEM((tm // chunk, LANES, chunk), F32),
        ],
        compiler_params=pltpu.CompilerParams(
            dimension_semantics=("arbitrary",), vmem_limit_bytes=VMEM_LIMIT),
        name="projmlstm",
    )(x2, g, w9, w_gg, w_gate, w_vt, bias_row, conv_q, conv_k, norm_b)


def _attn_kernel(slopes_ref, qi_ref, kj_ref, lq_ref, q_ref, k_ref, vt_ref, za_ref, na_ref, o_ref,
                 qs_sc, s_sc, m_sc, acc_sc, *, tq, nq, lam_init):
    h = pl.program_id(1)
    d = HEAD_DIM_A
    dv = 2 * d
    slope2 = slopes_ref[h] * LOG2E
    rows2 = 2 * tq
    n_ct = rows2 // LANES
    n_steps = nq * (nq + 1) // 2

    lane = lax.broadcasted_iota(jnp.int32, (tq, LANES), 1)
    lane2 = lax.broadcasted_iota(jnp.int32, (rows2, LANES), 1)
    ones_cols = jnp.where(lane2 < 3, 1.0, 0.0).astype(BF16)
    for qb in range(nq):
        q = (q_ref[qb * tq:(qb + 1) * tq, :].astype(F32) * (d ** -0.5 * LOG2E)).astype(BF16)
        zero = jnp.zeros_like(q)
        qs = jnp.concatenate([jnp.where(lane < d, q, zero), jnp.where(lane >= d, q, zero)], axis=0)
        qs_sc[qb] = jnp.concatenate([qs, ones_cols], axis=1)
    group = 2
    n_groups = n_ct // group
    gw = group * LANES
    groups_per_map = tq // gw

    def make_k_bias(r0, n):
        pos = lax.broadcasted_iota(jnp.int32, (n, LANES), 0) + r0
        ln = lax.broadcasted_iota(jnp.int32, (n, LANES), 1)
        b_hi, b_mid, b_lo = (part.astype(F32) for part in _split3(pos.astype(F32) * slope2))
        bias = jnp.where(ln == 0, b_hi, jnp.where(ln == 1, b_mid, jnp.where(ln == 2, b_lo, 0.0)))
        return bias.astype(BF16)

    k_bias = make_k_bias(0, tq)
    k_bias_parts = [make_k_bias(r0, gw) for r0 in range(0, tq, gw)]
    ones_rows = {n: jnp.where(lax.broadcasted_iota(jnp.int32, (2 * SUBLANES, n), 0) == 0, 1.0, 0.0).astype(BF16)
                 for n in range(gw, tq + 1, gw)}

    lq = lq_ref[...]
    lam = (jnp.exp(jnp.sum(lq[0:1] * lq[1:2], axis=1, keepdims=True))
           - jnp.exp(jnp.sum(lq[2:3] * lq[3:4], axis=1, keepdims=True)) + lam_init)

    def scores(t, slot, parts=None):
        k = k_ref[pl.ds(pl.multiple_of(kj_ref[t] * tq, tq), tq), :]
        k_aug = jnp.concatenate([k, k_bias], axis=1)
        for g in (range(n_groups) if parts is None else parts):
            rows = slice(g * group * LANES, (g + 1) * group * LANES)
            s = lax.dot_general(k_aug, qs_sc[qi_ref[t], rows, :], (((1,), (1,)), ((), ())),
                                preferred_element_type=F32)
            for c in range(group):
                s_sc[slot, g * group + c] = s[:, c * LANES:(c + 1) * LANES]

    def diag_rows(g):
        return (g % groups_per_map + 1) * gw

    def scores_diag(t, slot):
        i = qi_ref[t]
        for r0 in range(0, tq, gw):
            k = k_ref[pl.ds(pl.multiple_of(i * tq + r0, gw), gw), :]
            k_aug = jnp.concatenate([k, k_bias_parts[r0 // gw]], axis=1)
            gs = [g for g in range(n_groups) if diag_rows(g) > r0]
            for g in gs:
                s = lax.dot_general(k_aug, qs_sc[i, g * gw:(g + 1) * gw, :],
                                    (((1,), (1,)), ((), ())), preferred_element_type=F32)
                for c in range(group):
                    s_sc[slot, g * group + c, r0:r0 + gw, :] = s[:, c * LANES:(c + 1) * LANES]

    def value_product(j, ps, rows=tq):
        vt_aug = jnp.concatenate([vt_ref[j, :, :rows], ones_rows[rows]], axis=0)
        return jnp.dot(vt_aug, jnp.concatenate(ps, axis=1), preferred_element_type=F32)

    def first_update(i, slot, between=None):
        for c0 in range(0, n_ct, group):
            rows = diag_rows(c0 // group)
            ps = []
            for c in range(c0, c0 + group):
                cs = slice(c * LANES, (c + 1) * LANES)
                kv = lax.broadcasted_iota(jnp.int32, (rows, LANES), 0)
                qv = lax.broadcasted_iota(jnp.int32, (rows, LANES), 1) + (c * LANES) % tq
                s = jnp.where(kv <= qv, s_sc[slot, c, :rows, :], NEG)
                m_new = jnp.max(s, axis=0, keepdims=True)
                ps.append(jnp.exp2(s - m_new).astype(BF16))
                m_sc[i, :, cs] = m_new
            acc_sc[i, :, c0 * LANES:(c0 + group) * LANES] = value_product(i, ps, rows)
            if between is not None:
                between(c0 // group)

    def update(i, j, slot, between=None):
        off = ((j - i) * tq).astype(F32) * slope2
        for c0 in range(0, n_ct, group):
            gs = slice(c0 * LANES, (c0 + group) * LANES)
            ps, alphas = [], []
            for c in range(c0, c0 + group):
                cs = slice(c * LANES, (c + 1) * LANES)
                s = s_sc[slot, c]
                m_prev = m_sc[i, :, cs]
                m_new = jnp.maximum(m_prev, jnp.max(s, axis=0, keepdims=True) + off)
                alphas.append(jnp.exp2(m_prev - m_new))
                ps.append(jnp.exp2(s - (m_new - off)).astype(BF16))
                m_sc[i, :, cs] = m_new
            acc_sc[i, :, gs] = jnp.concatenate(alphas, axis=1) * acc_sc[i, :, gs] + value_product(j, ps)
            if between is not None:
                between(c0 // group)

    def finalize(i, carry):
        rows = pl.ds(pl.multiple_of(i * tq, tq), tq)
        ot = acc_sc[i, :dv, :] / acc_sc[i, dv:dv + 1, :]
        od = (ot[:, :tq] - lam * ot[:, tq:]).T
        ms = jnp.mean(od * od, axis=-1, keepdims=True)
        y = od * lax.rsqrt(ms + EPS) * na_ref[...] * (1.0 - lam_init)
        za = za_ref[rows, :].astype(F32)
        o_ref[rows, :] = (y * (za * _sigmoid(za))).astype(BF16)
        return carry

    unroll = s_sc.shape[0]
    assert nq % unroll == 0
    scores_diag(0, 0)

    def next_scores(t, u, diag=False):
        def emit(g):
            if g == 0:
                (scores_diag if diag else scores)(t + 1, (u + 1) % unroll)
        return emit

    def diag_body(tt, carry):
        for u in range(unroll):
            t = unroll * tt + u
            first_update(qi_ref[t], u, between=next_scores(t, u, diag=u + 1 < unroll))
        return carry

    lax.fori_loop(0, nq // unroll, diag_body, 0)

    n_trips = (n_steps - nq) // unroll
    assert (n_steps - nq) % unroll == 0 and n_trips == nq - 1 and unroll >= (nq - 1) / 2

    def off_body(tt, carry):
        for u in range(unroll):
            t = nq + unroll * tt + u
            update(qi_ref[t], kj_ref[t], u, between=next_scores(t, u))
            if u == 0:
                finalize(tt, carry)
        return carry

    lax.fori_loop(0, n_trips, off_body, 0)
    finalize(jnp.int32(nq - 1), 0)


def _attention(pa, vt4, slopes, lq, norm_a, *, lam_init, tq=512):
    b, _, s, _ = pa.shape
    h = N_HEADS_A
    nq = s // tq
    nt = vt4.shape[1]
    assert vt4.shape[3] == tq and nt == nq and nq % 2 == 0 and (nq * (nq - 1) // 2) % 2 == 0
    pairs = ([(i, i) for i in range(nq)] + [(i, j) for i in range(nq) for j in range(i)] + [(0, 0)])
    qi = jnp.asarray([p[0] for p in pairs], jnp.int32)
    kj = jnp.asarray([p[1] for p in pairs], jnp.int32)
    slab = lambda base: (lambda bi, hi: (bi, base + hi, 0, 0))
    smem = pl.BlockSpec(memory_space=pltpu.SMEM)
    return pl.pallas_call(
        functools.partial(_attn_kernel, tq=tq, nq=nq, lam_init=lam_init),
        grid=(b, h),
        in_specs=[
            smem, smem, smem,
            pl.BlockSpec((4, HEAD_DIM_A), lambda bi, hi: (0, 0)),
            pl.BlockSpec((None, None, s, LANES), slab(0)),
            pl.BlockSpec((None, None, s, LANES), slab(h)),
            pl.BlockSpec((None, nt, LANES, tq), lambda bi, hi: (bi, 0, hi, 0)),
            pl.BlockSpec((None, None, s, LANES), slab(2 * h)),
            pl.BlockSpec((1, LANES), lambda bi, hi: (0, hi)),
        ],
        out_specs=pl.BlockSpec((None, None, s, LANES), lambda bi, hi: (bi, hi, 0, 0)),
        out_shape=jax.ShapeDtypeStruct((b, h, s, LANES), BF16),
        scratch_shapes=[
            pltpu.VMEM((nq, 2 * tq, 2 * LANES), BF16),
            pltpu.VMEM((4, 2 * tq // LANES, tq, LANES), F32),
            pltpu.VMEM((nq, 1, 2 * tq), F32),
            pltpu.VMEM((nq, LANES + 2 * SUBLANES, 2 * tq), F32),
        ],
        compiler_params=pltpu.CompilerParams(
            dimension_semantics=("arbitrary", "arbitrary"), vmem_limit_bytes=VMEM_LIMIT),
        name="diffattn",
    )(slopes, qi, kj, lq, pa, pa, vt4, pa, norm_a)


def _merge_kernel(x_ref, ya_ref, yb_ref, ga_ref, gb_ref, wa_ref, wb_ref, wo_ref, g_ref, o_ref):
    ya = jnp.concatenate([ya_ref[hd] for hd in range(ya_ref.shape[0])], axis=1)
    a = jnp.dot(ya, wa_ref[...], preferred_element_type=F32)
    b = jnp.dot(yb_ref[...], wb_ref[...], preferred_element_type=F32)
    merged = _sigmoid(ga_ref[...].astype(F32)) * a + _sigmoid(gb_ref[...].astype(F32)) * b
    out = jnp.dot(merged.astype(BF16), wo_ref[...], preferred_element_type=F32)
    ms = jnp.mean(out * out, axis=-1, keepdims=True)
    o_ref[...] = x_ref[...] + out * lax.rsqrt(ms + EPS) * g_ref[...]


def _merge(x2, ya4, yb2, pg, w_a, w_b, w_out, layer, g, *, tm=1024):
    m, d = x2.shape
    _, ha, seq, _ = ya4.shape
    wa = yb2.shape[1]
    nt = seq // tm
    row = lambda i: (i, 0)
    const = lambda i: (0, 0)
    return pl.pallas_call(
        _merge_kernel,
        grid=(m // tm,),
        in_specs=[
            pl.BlockSpec((tm, d), row),
            pl.BlockSpec((None, ha, tm, LANES), lambda i: (i // nt, 0, i % nt, 0)),
            pl.BlockSpec((tm, wa), row),
            pl.BlockSpec((tm, d), lambda i: (i, 0)),
            pl.BlockSpec((tm, d), lambda i: (i, 1)),
            pl.BlockSpec((None, wa, d), lambda i: (layer, 0, 0)),
            pl.BlockSpec((None, wa, d), lambda i: (layer, 0, 0)),
            pl.BlockSpec((None, d, d), lambda i: (layer, 0, 0)),
            pl.BlockSpec((1, d), const),
        ],
        out_specs=pl.BlockSpec((tm, d), row),
        out_shape=jax.ShapeDtypeStruct((m, d), F32),
        compiler_params=pltpu.CompilerParams(
            dimension_semantics=("arbitrary",), vmem_limit_bytes=VMEM_LIMIT),
        name="merge",
    )(x2, ya4, yb2, pg, pg, w_a, w_b, w_out, g)


def _wprep_main_kernel(wt_ref, w_ref, wvt_ref, *, va_chunk):
    w_ref[...] = wt_ref[...].T.astype(BF16)

    @pl.when(pl.program_id(1) == va_chunk)
    def _():
        wvt_ref[...] = wt_ref[...].astype(BF16)


def _wprep_tail_kernel(lo_ref, hi_ref, w_ref, gate_ref, *, ng):
    tn = w_ref.shape[1]
    both = jnp.concatenate([lo_ref[...], hi_ref[:SUBLANES, :]], axis=0)
    w_ref[...] = both[ng:ng + tn].T.astype(BF16)

    @pl.when(pl.program_id(1) == 0)
    def _():
        head = lo_ref[:LANES, :]
        row = lax.broadcasted_iota(jnp.int32, head.shape, 0)
        gate_ref[...] = jnp.where(row < ng, head, 0.0).T.astype(BF16)


def _prep_weights(w_in, n9, ng, va_start, *, tn=512):
    depth, d, n_in = w_in.shape
    ngg = n_in - n9 - ng
    assert n9 % tn == 0 and ngg % tn == 0 and ng == SUBLANES and va_start % tn == 0
    w_t = jnp.transpose(w_in, (0, 2, 1))
    params = pltpu.CompilerParams(dimension_semantics=("arbitrary", "arbitrary"),
                                  vmem_limit_bytes=VMEM_LIMIT)
    w9, w_vt = pl.pallas_call(
        functools.partial(_wprep_main_kernel, va_chunk=va_start // tn),
        grid=(depth, n9 // tn),
        in_specs=[pl.BlockSpec((None, tn, d), lambda l, c: (l, c, 0))],
        out_specs=[pl.BlockSpec((None, d, tn), lambda l, c: (l, 0, c)),
                   pl.BlockSpec((None, tn, d), lambda l, c: (l, 0, 0))],
        out_shape=[jax.ShapeDtypeStruct((depth, d, n9), BF16),
                   jax.ShapeDtypeStruct((depth, tn, d), BF16)],
        compiler_params=params, name="wprep_main",
    )(w_t)
    first = n9 // tn
    w_gg, w_gate = pl.pallas_call(
        functools.partial(_wprep_tail_kernel, ng=ng),
        grid=(depth, ngg // tn),
        in_specs=[pl.BlockSpec((None, tn, d), lambda l, c: (l, first + c, 0)),
                  pl.BlockSpec((None, tn, d), lambda l, c: (l, first + c + 1, 0))],
        out_specs=[pl.BlockSpec((None, d, tn), lambda l, c: (l, 0, c)),
                   pl.BlockSpec((None, d, LANES), lambda l, c: (l, 0, 0))],
        out_shape=[jax.ShapeDtypeStruct((depth, d, ngg), BF16),
                   jax.ShapeDtypeStruct((depth, d, LANES), BF16)],
        compiler_params=params, name="wprep_tail",
    )(w_t, w_t)
    return w9, w_gg, w_gate, w_vt


def kernel(x, norm_pre, norm_post, w_in, b_if, conv_qk, lambda_qk, norm_a, norm_b, w_a, w_b, w_out):
    bsz, seq, d = x.shape
    depth = w_in.shape[0]
    wa = N_HEADS_A * 2 * HEAD_DIM_A
    wb = N_HEADS_B * HEAD_DIM_B
    n9 = 4 * wa + 5 * wb
    ng = 2 * N_HEADS_B
    assert w_in.shape[2] == n9 + ng + 2 * d and d == 1024 and wa == 512 and wb == 512
    slopes = jnp.asarray(2.0 ** (-8.0 * np.arange(1, N_HEADS_A + 1) / N_HEADS_A), F32)
    w9, w_gg, w_gate, w_vt = _prep_weights(w_in, n9, ng, va_start=2 * wa)
    w_a_b, w_b_b, w_out_b = w_a.astype(BF16), w_b.astype(BF16), w_out.astype(BF16)

    x2 = x.reshape(bsz * seq, d)
    for l in range(depth):
        bias_row = jnp.pad(b_if[l], (0, LANES - ng)).reshape(1, LANES)
        lam_init = 0.8 - 0.6 * math.exp(-0.3 * l)
        pg, pa, vt4, yb2 = _proj_mlstm(x2, norm_pre[l].reshape(1, d), w9, w_gg, w_gate, w_vt, l, bias_row,
                                   conv_qk[l][:, :wb], conv_qk[l][:, wb:], norm_b[l].reshape(1, wb),
                                   seq=seq)
        ya = _attention(pa, vt4, slopes, lambda_qk[l], norm_a[l].reshape(1, wa), lam_init=lam_init)
        x2 = _merge(x2, ya, yb2, pg,
                    w_a_b, w_b_b, w_out_b, l, norm_post[l].reshape(1, d))
    return x2.reshape(bsz, seq, d)
```

```python
import functools
import math

import jax
import jax.numpy as jnp
import numpy as np
from jax import lax
from jax.experimental import pallas as pl
from jax.experimental.pallas import tpu as pltpu

F32 = jnp.float32
BF16 = jnp.bfloat16

N_HEADS_A = 4
HEAD_DIM_A = 64
N_HEADS_B = 4
HEAD_DIM_B = 128
CONV_W = 4
EPS = 1e-6

LANES = 128
SUBLANES = 8
NEG = -1e30
LOG2E = math.log2(math.e)
VMEM_LIMIT = 56 * 1024 * 1024


def _sigmoid(x):
    return jax.nn.sigmoid(x)


def _split3(x):
    hi = x.astype(BF16)
    r1 = x - hi.astype(F32)
    mid = r1.astype(BF16)
    lo = (r1 - mid.astype(F32)).astype(BF16)
    return hi, mid, lo


def _shift_rows(x, tail, sh):
    r = pltpu.roll(x, sh, 0)
    rt = pltpu.roll(tail, sh, 0)
    row = lax.broadcasted_iota(jnp.int32, tail.shape, 0)
    first = jnp.where(row < sh, rt, r[:SUBLANES])
    return jnp.concatenate([first, r[SUBLANES:]], axis=0)


def _conv_silu(xb, tail_sc, w_ref):
    x = xb.astype(F32)
    tail = tail_sc[...]
    w = w_ref[...]
    y = x * w[CONV_W - 1:CONV_W]
    for tap in range(CONV_W - 1):
        y = y + _shift_rows(x, tail, CONV_W - 1 - tap) * w[tap:tap + 1]
    tail_sc[...] = x[x.shape[0] - SUBLANES:]
    return y * _sigmoid(y)


def _mlstm_gates(r0, L, pm_sc, g_sc, bias_ref, cq_ref, ck_ref, tq_sc, tk_sc, wt_sc):
    nh, dh = N_HEADS_B, HEAD_DIM_B
    wb = nh * dh
    rows = slice(r0, r0 + L)
    q_all = _conv_silu(pm_sc[rows, 0:wb], tq_sc, cq_ref)
    k_all = _conv_silu(pm_sc[rows, wb:2 * wb], tk_sc, ck_ref) * (dh ** -0.5)

    g_t = g_sc[:, rows] + bias_ref[...]
    lf_t = jnp.minimum(g_t, 0.0) - jnp.log1p(jnp.exp(-jnp.abs(g_t)))
    rr = lax.broadcasted_iota(jnp.int32, (L, L), 0)
    cc = lax.broadcasted_iota(jnp.int32, (L, L), 1)
    causal = cc <= rr
    tri_t = jnp.where(rr <= cc, 1.0, 0.0).astype(BF16)
    cs_t = sum(jnp.dot(part, tri_t, preferred_element_type=F32) for part in _split3(lf_t))
    gate_row = lax.broadcasted_iota(jnp.int32, g_t.shape, 0)
    w_t = jnp.where(gate_row < nh, g_t, cs_t)
    wt_sc[...] = w_t
    w = jnp.concatenate([w_t, jnp.zeros((LANES - w_t.shape[0], L), F32)], axis=0).T
    lane = lax.broadcasted_iota(jnp.int32, (L, LANES), 1)
    return q_all, k_all, w, causal, lane


def _mlstm_head(h, r0, L, gates, pm_sc, nb_ref, y_ref, c_sc, n_sc, m_sc, wt_sc):
    nh, dh = N_HEADS_B, HEAD_DIM_B
    wb = nh * dh
    rows = slice(r0, r0 + L)
    hs = slice(h * dh, (h + 1) * dh)
    q_all, k_all, w, causal, lane = gates
    w_last = w[L - 1:L, :]
    lane1 = lax.broadcasted_iota(jnp.int32, (1, LANES), 1)

    def pick(x, ln, idx):
        return jnp.sum(jnp.where(ln == idx, x, 0.0), axis=1, keepdims=True)

    b_col = pick(w, lane, nh + h)
    i_col = pick(w, lane, h)
    b_row = wt_sc[nh + h:nh + h + 1, :]
    i_row = wt_sc[h:h + 1, :]
    b_last = pick(w_last, lane1, nh + h)
    m_prev = jnp.max(m_sc[h:h + 1, :], axis=1, keepdims=True)

    log_d = jnp.where(causal, b_col + (i_row - b_row), NEG)
    inter = b_col + m_prev
    m_row = jnp.maximum(inter, jnp.max(log_d, axis=1, keepdims=True))
    d_m = jnp.exp(log_d - m_row)
    w_inter = jnp.exp(inter - m_row)

    q = q_all[:, hs]
    k = k_all[:, hs]
    qb = q.astype(BF16)
    v = pm_sc[rows, 2 * wb + h * dh:2 * wb + (h + 1) * dh]
    s_m = lax.dot_general(qb, k.astype(BF16), (((1,), (1,)), ((), ())),
                          preferred_element_type=F32) * d_m
    c_old = c_sc[h]
    n_old = n_sc[h]
    num = (jnp.dot(s_m.astype(BF16), v, preferred_element_type=F32)
           + w_inter * jnp.dot(qb, c_old.astype(BF16), preferred_element_type=F32))
    den = (jnp.sum(s_m, axis=1, keepdims=True)
           + w_inter * jnp.sum(q * n_old, axis=1, keepdims=True))
    hval = num / jnp.maximum(jnp.abs(den), jnp.exp(-m_row))

    a_col = b_last - b_col + i_col
    m_new = jnp.maximum(b_last + m_prev, jnp.max(a_col, axis=0, keepdims=True))
    decay = jnp.exp(b_last + m_prev - m_new)
    kw = k * jnp.exp(a_col - m_new)
    c_sc[h] = decay * c_old + jnp.dot(kw.T.astype(BF16), v, preferred_element_type=F32)
    n_sc[h] = decay * n_old + jnp.sum(kw, axis=0, keepdims=True)
    m_sc[h:h + 1, :] = jnp.broadcast_to(m_new, (1, LANES))

    o = pm_sc[rows, 3 * wb + h * dh:3 * wb + (h + 1) * dh].astype(F32)
    hb = _sigmoid(o) * hval
    ms = jnp.mean(hb * hb, axis=-1, keepdims=True)
    yb = hb * lax.rsqrt(ms + EPS) * nb_ref[:, hs]
    z = pm_sc[rows, 4 * wb + h * dh:4 * wb + (h + 1) * dh].astype(F32)
    y_ref[rows, hs] = (yb * (z * _sigmoid(z))).astype(BF16)


def _proj_kernel(x_ref, g_ref, w9_ref, wgg_ref, wg_ref, wvt_ref, bias_ref, cq_ref, ck_ref, nb_ref,
                 pg_ref, pa_ref, vt_ref, y_ref, pm_sc, g_sc, c_sc, n_sc, m_sc, tq_sc, tk_sc, wt_sc,
                 *, tn, chunk, steps_per_seq, attn_chunks, mlstm_chunks):
    @pl.when(pl.program_id(0) % steps_per_seq == 0)
    def _():
        c_sc[...] = jnp.zeros(c_sc.shape, F32)
        n_sc[...] = jnp.zeros(n_sc.shape, F32)
        m_sc[...] = jnp.zeros(m_sc.shape, F32)
        tq_sc[...] = jnp.zeros(tq_sc.shape, F32)
        tk_sc[...] = jnp.zeros(tk_sc.shape, F32)

    x = x_ref[...]
    ms = jnp.mean(x * x, axis=-1, keepdims=True)
    h = (x * lax.rsqrt(ms + EPS) * g_ref[...]).astype(BF16)
    n_chunks = x.shape[0] // chunk
    n_gg = wgg_ref.shape[1] // tn

    def project(w_ref, src):
        return jnp.dot(h, w_ref[:, src * tn:(src + 1) * tn], preferred_element_type=F32).astype(BF16)

    def to_pm(c):
        pm_sc[:, c * tn:(c + 1) * tn] = project(w9_ref, mlstm_chunks[c])

    def to_p_gg(c):
        pg_ref[:, c * tn:(c + 1) * tn] = project(wgg_ref, c)

    def to_p_attn(c):
        res = project(w9_ref, attn_chunks[c])
        per = tn // LANES
        for hd in range(per):
            pa_ref[c * per + hd] = res[:, hd * LANES:(hd + 1) * LANES]

    def to_vt():
        vt_ref[...] = lax.dot_general(wvt_ref[...], h, (((1,), (1,)), ((), ())),
                                      preferred_element_type=F32).astype(BF16)

    fillers = ([functools.partial(to_p_gg, c) for c in range(n_gg)]
               + [functools.partial(to_p_attn, c) for c in range(len(attn_chunks))] + [to_vt])
    to_pm(0)
    to_pm(1)
    g_sc[...] = lax.dot_general(wg_ref[...], h, (((1,), (1,)), ((), ())), preferred_element_type=F32)
    gates = []
    for ci in range(n_chunks):
        gates.append(_mlstm_gates(ci * chunk, chunk, pm_sc, g_sc, bias_ref, cq_ref, ck_ref,
                                  tq_sc, tk_sc, wt_sc.at[ci]))
        if 2 + ci < len(mlstm_chunks):
            to_pm(2 + ci)
    for c in range(2 + n_chunks, len(mlstm_chunks)):
        to_pm(c)
    for ci in range(n_chunks):
        for hd in range(N_HEADS_B):
            if fillers:
                fillers.pop(0)()
            _mlstm_head(hd, ci * chunk, chunk, gates[ci], pm_sc, nb_ref, y_ref,
                        c_sc, n_sc, m_sc, wt_sc.at[ci])
    for filler in fillers:
        filler()


def _proj_mlstm(x2, g, w9, w_gg, w_gate, w_vt, layer, bias_row, conv_q, conv_k, norm_b,
                *, seq, tm=512, tn=512, chunk=256):
    m, d = x2.shape
    n9 = w9.shape[2]
    ngg = w_gg.shape[2]
    wv = w_vt.shape[1]
    wb = N_HEADS_B * HEAD_DIM_B
    attn_chunks, mlstm_chunks = (0, 1, 3), (4, 5, 6, 7, 8)
    assert n9 == 9 * tn and wb == tn and wv == tn
    n_slabs = len(attn_chunks) * tn // LANES
    nt = seq // tm
    layer_blk = lambda i: (layer, 0, 0)
    const = lambda i: (0, 0)
    once = dict(pipeline_mode=pl.Buffered(1))
    return pl.pallas_call(
        functools.partial(_proj_kernel, tn=tn, chunk=chunk, steps_per_seq=nt,
                          attn_chunks=attn_chunks, mlstm_chunks=mlstm_chunks),
        grid=(m // tm,),
        in_specs=[
            pl.BlockSpec((tm, d), lambda i: (i, 0)),
            pl.BlockSpec((1, d), const),
            pl.BlockSpec((None, d, n9), layer_blk, **once),
            pl.BlockSpec((None, d, ngg), layer_blk, **once),
            pl.BlockSpec((None, 2 * SUBLANES, d), layer_blk),
            pl.BlockSpec((None, wv, d), layer_blk),
            pl.BlockSpec((2 * SUBLANES, chunk), const),
            pl.BlockSpec((CONV_W, wb), const),
            pl.BlockSpec((CONV_W, wb), const),
            pl.BlockSpec((1, wb), const),
        ],
        out_specs=[
            pl.BlockSpec((tm, ngg), lambda i: (i, 0)),
            pl.BlockSpec((None, n_slabs, tm, LANES), lambda i: (i // nt, 0, i % nt, 0)),
            pl.BlockSpec((None, None, wv, tm), lambda i: (i // nt, i % nt, 0, 0)),
            pl.BlockSpec((tm, wb), lambda i: (i, 0)),
        ],
        out_shape=[
            jax.ShapeDtypeStruct((m, ngg), BF16),
            jax.ShapeDtypeStruct((m // seq, n_slabs, seq, LANES), BF16),
            jax.ShapeDtypeStruct((m // seq, nt, wv, tm), BF16),
            jax.ShapeDtypeStruct((m, wb), BF16),
        ],
        scratch_shapes=[
            pltpu.VMEM((tm, len(mlstm_chunks) * tn), BF16),
            pltpu.VMEM((2 * SUBLANES, tm), F32),
            pltpu.VMEM((N_HEADS_B, HEAD_DIM_B, HEAD_DIM_B), F32),
            pltpu.VMEM((N_HEADS_B, 1, HEAD_DIM_B), F32),
            pltpu.VMEM((SUBLANES, LANES), F32),
            pltpu.VMEM((SUBLANES, wb), F32),
            pltpu.VMEM((SUBLANES, wb), F32),
            pltpu.VMEM((tm // chunk, 2 * SUBLANES, chunk), F32),
        ],
        compiler_params=pltpu.CompilerParams(
            dimension_semantics=("arbitrary",), vmem_limit_bytes=VMEM_LIMIT),
        name="projmlstm",
    )(x2, g, w9, w_gg, w_gate, w_vt, jnp.broadcast_to(bias_row[:, None], (2 * SUBLANES, chunk)),
      conv_q, conv_k, norm_b)


def _attn_kernel(slopes_ref, qi_ref, kj_ref, lq_ref, q_ref, k_ref, vt_ref, za_ref, na_ref, o_ref,
                 qs_sc, s_sc, m_sc, acc_sc, *, tq, nq, lam_init):
    h = pl.program_id(1)
    d = HEAD_DIM_A
    dv = 2 * d
    slope2 = slopes_ref[h] * LOG2E
    rows2 = 2 * tq
    n_ct = rows2 // LANES
    n_steps = nq * (nq + 1) // 2

    lane = lax.broadcasted_iota(jnp.int32, (tq, LANES), 1)
    lane2 = lax.broadcasted_iota(jnp.int32, (rows2, LANES), 1)
    ones_cols = jnp.where(lane2 < 3, 1.0, 0.0).astype(BF16)
    for qb in range(nq):
        q = q_ref[qb * tq:(qb + 1) * tq, :]
        zero = jnp.zeros_like(q)
        qs = jnp.concatenate([jnp.where(lane < d, q, zero), jnp.where(lane >= d, q, zero)], axis=0)
        qs_sc[qb] = jnp.concatenate([qs, ones_cols], axis=1)
    group = 2
    n_groups = n_ct // group
    gw = group * LANES
    groups_per_map = tq // gw

    def make_k_bias(r0, n):
        pos = lax.broadcasted_iota(jnp.int32, (n, LANES), 0) + r0
        ln = lax.broadcasted_iota(jnp.int32, (n, LANES), 1)
        b_hi, b_mid, b_lo = (part.astype(F32) for part in _split3(pos.astype(F32) * slope2))
        bias = jnp.where(ln == 0, b_hi, jnp.where(ln == 1, b_mid, jnp.where(ln == 2, b_lo, 0.0)))
        return bias.astype(BF16)

    k_bias = make_k_bias(0, tq)
    k_bias_parts = [make_k_bias(r0, gw) for r0 in range(0, tq, gw)]
    ones_rows = {n: jnp.where(lax.broadcasted_iota(jnp.int32, (2 * SUBLANES, n), 0) == 0, 1.0, 0.0).astype(BF16)
                 for n in range(gw, tq + 1, gw)}

    lq = lq_ref[...]
    lam = (jnp.exp(jnp.sum(lq[0:1] * lq[1:2], axis=1, keepdims=True))
           - jnp.exp(jnp.sum(lq[2:3] * lq[3:4], axis=1, keepdims=True)) + lam_init)

    def scores(t, slot, parts=None):
        k = k_ref[pl.ds(pl.multiple_of(kj_ref[t] * tq, tq), tq), :]
        k_aug = jnp.concatenate([k, k_bias], axis=1)
        for g in (range(n_groups) if parts is None else parts):
            rows = slice(g * group * LANES, (g + 1) * group * LANES)
            s = lax.dot_general(k_aug, qs_sc[qi_ref[t], rows, :], (((1,), (1,)), ((), ())),
                                preferred_element_type=F32)
            for c in range(group):
                s_sc[slot, g * group + c] = s[:, c * LANES:(c + 1) * LANES]

    def diag_rows(g):
        return (g % groups_per_map + 1) * gw

    def scores_diag(t, slot):
        i = qi_ref[t]
        for r0 in range(0, tq, gw):
            k = k_ref[pl.ds(pl.multiple_of(i * tq + r0, gw), gw), :]
            k_aug = jnp.concatenate([k, k_bias_parts[r0 // gw]], axis=1)
            gs = [g for g in range(n_groups) if diag_rows(g) > r0]
            for g in gs:
                s = lax.dot_general(k_aug, qs_sc[i, g * gw:(g + 1) * gw, :],
                                    (((1,), (1,)), ((), ())), preferred_element_type=F32)
                for c in range(group):
                    s_sc[slot, g * group + c, r0:r0 + gw, :] = s[:, c * LANES:(c + 1) * LANES]

    def value_product(j, ps, rows=tq):
        vt_aug = jnp.concatenate([vt_ref[j, :, :rows], ones_rows[rows]], axis=0)
        return jnp.dot(vt_aug, jnp.concatenate(ps, axis=1), preferred_element_type=F32)

    def first_update(i, slot, between=None):
        for c0 in range(0, n_ct, group):
            rows = diag_rows(c0 // group)
            ps = []
            for c in range(c0, c0 + group):
                cs = slice(c * LANES, (c + 1) * LANES)
                kv = lax.broadcasted_iota(jnp.int32, (rows, LANES), 0)
                qv = lax.broadcasted_iota(jnp.int32, (rows, LANES), 1) + (c * LANES) % tq
                s = jnp.where(kv <= qv, s_sc[slot, c, :rows, :], NEG)
                m_new = jnp.max(s, axis=0, keepdims=True)
                ps.append(jnp.exp2(s - m_new).astype(BF16))
                m_sc[i, :, cs] = m_new
            acc_sc[i, :, c0 * LANES:(c0 + group) * LANES] = value_product(i, ps, rows)
            if between is not None:
                between(c0 // group)

    def update(i, j, slot, between=None):
        off = ((j - i) * tq).astype(F32) * slope2
        for c0 in range(0, n_ct, group):
            gs = slice(c0 * LANES, (c0 + group) * LANES)
            ps, alphas = [], []
            for c in range(c0, c0 + group):
                cs = slice(c * LANES, (c + 1) * LANES)
                s = s_sc[slot, c]
                m_prev = m_sc[i, :, cs]
                m_new = jnp.maximum(m_prev, jnp.max(s, axis=0, keepdims=True) + off)
                alphas.append(jnp.exp2(m_prev - m_new))
                ps.append(jnp.exp2(s - (m_new - off)).astype(BF16))
                m_sc[i, :, cs] = m_new
            acc_sc[i, :, gs] = jnp.concatenate(alphas, axis=1) * acc_sc[i, :, gs] + value_product(j, ps)
            if between is not None:
                between(c0 // group)

    def finalize(i, carry):
        rows = pl.ds(pl.multiple_of(i * tq, tq), tq)
        ot = acc_sc[i, :dv, :] / acc_sc[i, dv:dv + 1, :]
        od = (ot[:, :tq] - lam * ot[:, tq:]).T
        ms = jnp.mean(od * od, axis=-1, keepdims=True)
        y = od * lax.rsqrt(ms + EPS) * na_ref[...] * (1.0 - lam_init)
        za = za_ref[rows, :].astype(F32)
        o_ref[rows, :] = (y * (za * _sigmoid(za))).astype(BF16)
        return carry

    unroll = s_sc.shape[0]
    assert nq % unroll == 0
    scores_diag(0, 0)

    def next_scores(t, u, diag=False):
        def emit(g):
            if g == 0:
                (scores_diag if diag else scores)(t + 1, (u + 1) % unroll)
        return emit

    def diag_body(tt, carry):
        for u in range(unroll):
            t = unroll * tt + u
            first_update(qi_ref[t], u, between=next_scores(t, u, diag=u + 1 < unroll))
        return carry

    lax.fori_loop(0, nq // unroll, diag_body, 0)

    n_trips = (n_steps - nq) // unroll
    assert (n_steps - nq) % unroll == 0 and n_trips == nq - 1 and unroll >= (nq - 1) / 2

    def off_body(tt, carry):
        for u in range(unroll):
            t = nq + unroll * tt + u
            update(qi_ref[t], kj_ref[t], u, between=next_scores(t, u))
            if u == 0:
                finalize(tt, carry)
        return carry

    lax.fori_loop(0, n_trips, off_body, 0)
    finalize(jnp.int32(nq - 1), 0)


def _attention(pa, vt4, slopes, lq, norm_a, *, lam_init, tq=512):
    b, _, s, _ = pa.shape
    h = N_HEADS_A
    nq = s // tq
    nt = vt4.shape[1]
    assert vt4.shape[3] == tq and nt == nq and nq % 2 == 0 and (nq * (nq - 1) // 2) % 2 == 0
    pairs = ([(i, i) for i in range(nq)] + [(i, j) for i in range(nq) for j in range(i)] + [(0, 0)])
    qi = jnp.asarray([p[0] for p in pairs], jnp.int32)
    kj = jnp.asarray([p[1] for p in pairs], jnp.int32)
    slab = lambda base: (lambda bi, hi: (bi, base + hi, 0, 0))
    smem = pl.BlockSpec(memory_space=pltpu.SMEM)
    return pl.pallas_call(
        functools.partial(_attn_kernel, tq=tq, nq=nq, lam_init=lam_init),
        grid=(b, h),
        in_specs=[
            smem, smem, smem,
            pl.BlockSpec((4, HEAD_DIM_A), lambda bi, hi: (0, 0)),
            pl.BlockSpec((None, None, s, LANES), slab(0)),
            pl.BlockSpec((None, None, s, LANES), slab(h)),
            pl.BlockSpec((None, nt, LANES, tq), lambda bi, hi: (bi, 0, hi, 0)),
            pl.BlockSpec((None, None, s, LANES), slab(2 * h)),
            pl.BlockSpec((1, LANES), lambda bi, hi: (0, hi)),
        ],
        out_specs=pl.BlockSpec((None, None, s, LANES), lambda bi, hi: (bi, hi, 0, 0)),
        out_shape=jax.ShapeDtypeStruct((b, h, s, LANES), BF16),
        scratch_shapes=[
            pltpu.VMEM((nq, 2 * tq, 2 * LANES), BF16),
            pltpu.VMEM((4, 2 * tq // LANES, tq, LANES), F32),
            pltpu.VMEM((nq, 1, 2 * tq), F32),
            pltpu.VMEM((nq, LANES + 2 * SUBLANES, 2 * tq), F32),
        ],
        compiler_params=pltpu.CompilerParams(
            dimension_semantics=("arbitrary", "arbitrary"), vmem_limit_bytes=VMEM_LIMIT),
        name="diffattn",
    )(slopes, qi, kj, lq, pa, pa, vt4, pa, norm_a)


def _merge_kernel(x_ref, ya_ref, yb_ref, ga_ref, gb_ref, wa_ref, wb_ref, wo_ref, g_ref, o_ref):
    ya = jnp.concatenate([ya_ref[hd] for hd in range(ya_ref.shape[0])], axis=1)
    a = jnp.dot(ya, wa_ref[...], preferred_element_type=F32)
    b = jnp.dot(yb_ref[...], wb_ref[...], preferred_element_type=F32)
    merged = _sigmoid(ga_ref[...].astype(F32)) * a + _sigmoid(gb_ref[...].astype(F32)) * b
    out = jnp.dot(merged.astype(BF16), wo_ref[...], preferred_element_type=F32)
    ms = jnp.mean(out * out, axis=-1, keepdims=True)
    o_ref[...] = x_ref[...] + out * lax.rsqrt(ms + EPS) * g_ref[...]


def _merge(x2, ya4, yb2, pg, w_a, w_b, w_out, layer, g, *, tm=1024):
    m, d = x2.shape
    _, ha, seq, _ = ya4.shape
    wa = yb2.shape[1]
    nt = seq // tm
    row = lambda i: (i, 0)
    const = lambda i: (0, 0)
    return pl.pallas_call(
        _merge_kernel,
        grid=(m // tm,),
        in_specs=[
            pl.BlockSpec((tm, d), row),
            pl.BlockSpec((None, ha, tm, LANES), lambda i: (i // nt, 0, i % nt, 0)),
            pl.BlockSpec((tm, wa), row),
            pl.BlockSpec((tm, d), lambda i: (i, 0)),
            pl.BlockSpec((tm, d), lambda i: (i, 1)),
            pl.BlockSpec((None, wa, d), lambda i: (layer, 0, 0)),
            pl.BlockSpec((None, wa, d), lambda i: (layer, 0, 0)),
            pl.BlockSpec((None, d, d), lambda i: (layer, 0, 0)),
            pl.BlockSpec((1, d), const),
        ],
        out_specs=pl.BlockSpec((tm, d), row),
        out_shape=jax.ShapeDtypeStruct((m, d), F32),
        compiler_params=pltpu.CompilerParams(
            dimension_semantics=("arbitrary",), vmem_limit_bytes=VMEM_LIMIT),
        name="merge",
    )(x2, ya4, yb2, pg, pg, w_a, w_b, w_out, g)


def _wprep_main_kernel(wt_ref, w_ref, wvt_ref, *, va_chunk, qa_chunk, qa_scale):
    scale = jnp.where(pl.program_id(1) == qa_chunk, qa_scale, 1.0).astype(F32)
    w_ref[...] = (wt_ref[...] * scale).T.astype(BF16)

    @pl.when(pl.program_id(1) == va_chunk)
    def _():
        wvt_ref[...] = wt_ref[...].astype(BF16)


def _wprep_tail_kernel(lo_ref, hi_ref, w_ref, gate_ref, *, ng):
    tn = w_ref.shape[1]
    both = jnp.concatenate([lo_ref[...], hi_ref[:SUBLANES, :]], axis=0)
    w_ref[...] = both[ng:ng + tn].T.astype(BF16)

    @pl.when(pl.program_id(1) == 0)
    def _():
        head = lo_ref[:2 * SUBLANES, :]
        row = lax.broadcasted_iota(jnp.int32, head.shape, 0)
        gate_ref[...] = jnp.where(row < ng, head, 0.0).astype(BF16)


def _prep_weights(w_in, n9, ng, va_start, qa_width, qa_scale, *, tn=512):
    depth, d, n_in = w_in.shape
    ngg = n_in - n9 - ng
    assert n9 % tn == 0 and ngg % tn == 0 and ng == SUBLANES and va_start % tn == 0 and qa_width == tn
    w_t = jnp.transpose(w_in, (0, 2, 1))
    params = pltpu.CompilerParams(dimension_semantics=("arbitrary", "arbitrary"),
                                  vmem_limit_bytes=VMEM_LIMIT)
    w9, w_vt = pl.pallas_call(
        functools.partial(_wprep_main_kernel, va_chunk=va_start // tn, qa_chunk=0, qa_scale=qa_scale),
        grid=(depth, n9 // tn),
        in_specs=[pl.BlockSpec((None, tn, d), lambda l, c: (l, c, 0))],
        out_specs=[pl.BlockSpec((None, d, tn), lambda l, c: (l, 0, c)),
                   pl.BlockSpec((None, tn, d), lambda l, c: (l, 0, 0))],
        out_shape=[jax.ShapeDtypeStruct((depth, d, n9), BF16),
                   jax.ShapeDtypeStruct((depth, tn, d), BF16)],
        compiler_params=params, name="wprep_main",
    )(w_t)
    first = n9 // tn
    w_gg, w_gate = pl.pallas_call(
        functools.partial(_wprep_tail_kernel, ng=ng),
        grid=(depth, ngg // tn),
        in_specs=[pl.BlockSpec((None, tn, d), lambda l, c: (l, first + c, 0)),
                  pl.BlockSpec((None, tn, d), lambda l, c: (l, first + c + 1, 0))],
        out_specs=[pl.BlockSpec((None, d, tn), lambda l, c: (l, 0, c)),
                   pl.BlockSpec((None, 2 * SUBLANES, d), lambda l, c: (l, 0, 0))],
        out_shape=[jax.ShapeDtypeStruct((depth, d, ngg), BF16),
                   jax.ShapeDtypeStruct((depth, 2 * SUBLANES, d), BF16)],
        compiler_params=params, name="wprep_tail",
    )(w_t, w_t)
    return w9, w_gg, w_gate, w_vt


def kernel(x, norm_pre, norm_post, w_in, b_if, conv_qk, lambda_qk, norm_a, norm_b, w_a, w_b, w_out):
    bsz, seq, d = x.shape
    depth = w_in.shape[0]
    wa = N_HEADS_A * 2 * HEAD_DIM_A
    wb = N_HEADS_B * HEAD_DIM_B
    n9 = 4 * wa + 5 * wb
    ng = 2 * N_HEADS_B
    assert w_in.shape[2] == n9 + ng + 2 * d and d == 1024 and wa == 512 and wb == 512
    slopes = jnp.asarray(2.0 ** (-8.0 * np.arange(1, N_HEADS_A + 1) / N_HEADS_A), F32)
    w9, w_gg, w_gate, w_vt = _prep_weights(w_in, n9, ng, va_start=2 * wa, qa_width=wa,
                                           qa_scale=HEAD_DIM_A ** -0.5 * LOG2E)
    w_a_b, w_b_b, w_out_b = w_a.astype(BF16), w_b.astype(BF16), w_out.astype(BF16)

    x2 = x.reshape(bsz * seq, d)
    for l in range(depth):
        bias_row = jnp.pad(b_if[l], (0, 2 * SUBLANES - ng))
        lam_init = 0.8 - 0.6 * math.exp(-0.3 * l)
        pg, pa, vt4, yb2 = _proj_mlstm(x2, norm_pre[l].reshape(1, d), w9, w_gg, w_gate, w_vt, l, bias_row,
                                   conv_qk[l][:, :wb], conv_qk[l][:, wb:], norm_b[l].reshape(1, wb),
                                   seq=seq)
        ya = _attention(pa, vt4, slopes, lambda_qk[l], norm_a[l].reshape(1, wa), lam_init=lam_init)
        x2 = _merge(x2, ya, yb2, pg,
                    w_a_b, w_b_b, w_out_b, l, norm_post[l].reshape(1, d))
    return x2.reshape(bsz, seq, d)
```

```python
import functools
import math

import jax
import jax.numpy as jnp
import numpy as np
from jax import lax
from jax.experimental import pallas as pl
from jax.experimental.pallas import tpu as pltpu

F32 = jnp.float32
BF16 = jnp.bfloat16

N_HEADS_A = 4
HEAD_DIM_A = 64
N_HEADS_B = 4
HEAD_DIM_B = 128
CONV_W = 4
EPS = 1e-6

LANES = 128
SUBLANES = 8
NEG = -1e30
LOG2E = math.log2(math.e)
VMEM_LIMIT = 56 * 1024 * 1024


def _sigmoid(x):
    return jax.nn.sigmoid(x)


def _split3(x):
    hi = x.astype(BF16)
    r1 = x - hi.astype(F32)
    mid = r1.astype(BF16)
    lo = (r1 - mid.astype(F32)).astype(BF16)
    return hi, mid, lo


def _shift_rows(x, tail, sh):
    r = pltpu.roll(x, sh, 0)
    rt = pltpu.roll(tail, sh, 0)
    row = lax.broadcasted_iota(jnp.int32, tail.shape, 0)
    first = jnp.where(row < sh, rt, r[:SUBLANES])
    return jnp.concatenate([first, r[SUBLANES:]], axis=0)


def _conv_silu(xb, tail_sc, w_ref):
    x = xb.astype(F32)
    tail = tail_sc[...]
    w = w_ref[...]
    y = x * w[CONV_W - 1:CONV_W]
    for tap in range(CONV_W - 1):
        y = y + _shift_rows(x, tail, CONV_W - 1 - tap) * w[tap:tap + 1]
    tail_sc[...] = x[x.shape[0] - SUBLANES:]
    return y * _sigmoid(y)


def _mlstm_gates(r0, L, pm_sc, g_sc, bias_ref, cq_ref, ck_ref, tq_sc, tk_sc, wt_sc):
    nh, dh = N_HEADS_B, HEAD_DIM_B
    wb = nh * dh
    rows = slice(r0, r0 + L)
    q_all = _conv_silu(pm_sc[rows, 0:wb], tq_sc, cq_ref)
    k_all = _conv_silu(pm_sc[rows, wb:2 * wb], tk_sc, ck_ref) * (dh ** -0.5)

    g_t = g_sc[:, rows] + bias_ref[...]
    lf_t = jnp.minimum(g_t, 0.0) - jnp.log1p(jnp.exp(-jnp.abs(g_t)))
    rr = lax.broadcasted_iota(jnp.int32, (L, L), 0)
    cc = lax.broadcasted_iota(jnp.int32, (L, L), 1)
    causal = cc <= rr
    tri_t = jnp.where(rr <= cc, 1.0, 0.0).astype(BF16)
    cs_t = sum(jnp.dot(part, tri_t, preferred_element_type=F32) for part in _split3(lf_t))
    gate_row = lax.broadcasted_iota(jnp.int32, g_t.shape, 0)
    w_t = jnp.where(gate_row < nh, g_t, cs_t)
    wt_sc[...] = w_t
    w = jnp.concatenate([w_t, jnp.zeros((LANES - w_t.shape[0], L), F32)], axis=0).T
    lane = lax.broadcasted_iota(jnp.int32, (L, LANES), 1)
    return q_all, k_all, w, causal, lane


def _mlstm_head(h, r0, L, gates, pm_sc, nb_ref, y_ref, c_sc, n_sc, m_sc, wt_sc):
    nh, dh = N_HEADS_B, HEAD_DIM_B
    wb = nh * dh
    rows = slice(r0, r0 + L)
    hs = slice(h * dh, (h + 1) * dh)
    q_all, k_all, w, causal, lane = gates
    w_last = w[L - 1:L, :]
    lane1 = lax.broadcasted_iota(jnp.int32, (1, LANES), 1)

    def pick(x, ln, idx):
        return jnp.sum(jnp.where(ln == idx, x, 0.0), axis=1, keepdims=True)

    b_col = pick(w, lane, nh + h)
    i_col = pick(w, lane, h)
    b_row = wt_sc[nh + h:nh + h + 1, :]
    i_row = wt_sc[h:h + 1, :]
    b_last = pick(w_last, lane1, nh + h)
    m_prev = jnp.max(m_sc[h:h + 1, :], axis=1, keepdims=True)

    log_d = jnp.where(causal, b_col + (i_row - b_row), NEG)
    inter = b_col + m_prev
    m_row = jnp.maximum(inter, jnp.max(log_d, axis=1, keepdims=True))
    d_m = jnp.exp(log_d - m_row)
    w_inter = jnp.exp(inter - m_row)

    q = q_all[:, hs]
    k = k_all[:, hs]
    qb = q.astype(BF16)
    v = pm_sc[rows, 2 * wb + h * dh:2 * wb + (h + 1) * dh]
    s_m = lax.dot_general(qb, k.astype(BF16), (((1,), (1,)), ((), ())),
                          preferred_element_type=F32) * d_m
    c_old = c_sc[h]
    n_old = n_sc[h]
    num = (jnp.dot(s_m.astype(BF16), v, preferred_element_type=F32)
           + w_inter * jnp.dot(qb, c_old.astype(BF16), preferred_element_type=F32))
    den = (jnp.sum(s_m, axis=1, keepdims=True)
           + w_inter * jnp.sum(q * n_old, axis=1, keepdims=True))
    hval = num / jnp.maximum(jnp.abs(den), jnp.exp(-m_row))

    a_col = b_last - b_col + i_col
    m_new = jnp.maximum(b_last + m_prev, jnp.max(a_col, axis=0, keepdims=True))
    decay = jnp.exp(b_last + m_prev - m_new)
    kw = k * jnp.exp(a_col - m_new)
    c_sc[h] = decay * c_old + jnp.dot(kw.T.astype(BF16), v, preferred_element_type=F32)
    n_sc[h] = decay * n_old + jnp.sum(kw, axis=0, keepdims=True)
    m_sc[h:h + 1, :] = jnp.broadcast_to(m_new, (1, LANES))

    o = pm_sc[rows, 3 * wb + h * dh:3 * wb + (h + 1) * dh].astype(F32)
    hb = _sigmoid(o) * hval
    ms = jnp.mean(hb * hb, axis=-1, keepdims=True)
    yb = hb * lax.rsqrt(ms + EPS) * nb_ref[:, hs]
    z = pm_sc[rows, 4 * wb + h * dh:4 * wb + (h + 1) * dh].astype(F32)
    y_ref[rows, hs] = (yb * (z * _sigmoid(z))).astype(BF16)


def _proj_kernel(x_ref, g_ref, w9_ref, wgg_ref, wg_ref, wvt_ref, bias_ref, cq_ref, ck_ref, nb_ref,
                 pg_ref, pa_ref, vt_ref, y_ref, pm_sc, g_sc, c_sc, n_sc, m_sc, tq_sc, tk_sc, wt_sc,
                 *, tn, chunk, steps_per_seq, attn_chunks, mlstm_chunks):
    @pl.when(pl.program_id(0) % steps_per_seq == 0)
    def _():
        c_sc[...] = jnp.zeros(c_sc.shape, F32)
        n_sc[...] = jnp.zeros(n_sc.shape, F32)
        m_sc[...] = jnp.zeros(m_sc.shape, F32)
        tq_sc[...] = jnp.zeros(tq_sc.shape, F32)
        tk_sc[...] = jnp.zeros(tk_sc.shape, F32)

    x = x_ref[...]
    ms = jnp.mean(x * x, axis=-1, keepdims=True)
    h = (x * lax.rsqrt(ms + EPS) * g_ref[...]).astype(BF16)
    n_chunks = x.shape[0] // chunk
    n_gg = wgg_ref.shape[1] // tn

    def project(w_ref, src):
        return jnp.dot(h, w_ref[:, src * tn:(src + 1) * tn], preferred_element_type=F32).astype(BF16)

    def to_pm(c):
        pm_sc[:, c * tn:(c + 1) * tn] = project(w9_ref, mlstm_chunks[c])

    def to_p_gg(c):
        pg_ref[:, c * tn:(c + 1) * tn] = project(wgg_ref, c)

    def to_p_attn(c):
        res = project(w9_ref, attn_chunks[c])
        per = tn // LANES
        for hd in range(per):
            pa_ref[c * per + hd] = res[:, hd * LANES:(hd + 1) * LANES]

    def to_vt():
        vt_ref[...] = lax.dot_general(wvt_ref[...], h, (((1,), (1,)), ((), ())),
                                      preferred_element_type=F32).astype(BF16)

    fillers = ([functools.partial(to_p_gg, c) for c in range(n_gg)]
               + [functools.partial(to_p_attn, c) for c in range(len(attn_chunks))] + [to_vt])
    to_pm(0)
    to_pm(1)
    g_sc[...] = lax.dot_general(wg_ref[...], h, (((1,), (1,)), ((), ())), preferred_element_type=F32)
    gates = []
    for ci in range(n_chunks):
        if 2 + ci < len(mlstm_chunks):
            to_pm(2 + ci)
        gates.append(_mlstm_gates(ci * chunk, chunk, pm_sc, g_sc, bias_ref, cq_ref, ck_ref,
                                  tq_sc, tk_sc, wt_sc.at[ci]))
    for c in range(2 + n_chunks, len(mlstm_chunks)):
        to_pm(c)
    for ci in range(n_chunks):
        for hd in range(N_HEADS_B):
            _mlstm_head(hd, ci * chunk, chunk, gates[ci], pm_sc, nb_ref, y_ref,
                        c_sc, n_sc, m_sc, wt_sc.at[ci])
            if fillers:
                fillers.pop(0)()
    for filler in fillers:
        filler()


def _proj_mlstm(x2, g, w9, w_gg, w_gate, w_vt, layer, bias_row, conv_q, conv_k, norm_b,
                *, seq, tm=512, tn=512, chunk=256):
    m, d = x2.shape
    n9 = w9.shape[2]
    ngg = w_gg.shape[2]
    wv = w_vt.shape[1]
    wb = N_HEADS_B * HEAD_DIM_B
    attn_chunks, mlstm_chunks = (0, 1, 3), (4, 5, 6, 7, 8)
    assert n9 == 9 * tn and wb == tn and wv == tn
    n_slabs = len(attn_chunks) * tn // LANES
    nt = seq // tm
    layer_blk = lambda i: (layer, 0, 0)
    const = lambda i: (0, 0)
    once = dict(pipeline_mode=pl.Buffered(1))
    return pl.pallas_call(
        functools.partial(_proj_kernel, tn=tn, chunk=chunk, steps_per_seq=nt,
                          attn_chunks=attn_chunks, mlstm_chunks=mlstm_chunks),
        grid=(m // tm,),
        in_specs=[
            pl.BlockSpec((tm, d), lambda i: (i, 0)),
            pl.BlockSpec((1, d), const),
            pl.BlockSpec((None, d, n9), layer_blk, **once),
            pl.BlockSpec((None, d, ngg), layer_blk, **once),
            pl.BlockSpec((None, 2 * SUBLANES, d), layer_blk),
            pl.BlockSpec((None, wv, d), layer_blk),
            pl.BlockSpec((2 * SUBLANES, chunk), const),
            pl.BlockSpec((CONV_W, wb), const),
            pl.BlockSpec((CONV_W, wb), const),
            pl.BlockSpec((1, wb), const),
        ],
        out_specs=[
            pl.BlockSpec((tm, ngg), lambda i: (i, 0)),
            pl.BlockSpec((None, n_slabs, tm, LANES), lambda i: (i // nt, 0, i % nt, 0)),
            pl.BlockSpec((None, None, wv, tm), lambda i: (i // nt, i % nt, 0, 0)),
            pl.BlockSpec((tm, wb), lambda i: (i, 0)),
        ],
        out_shape=[
            jax.ShapeDtypeStruct((m, ngg), BF16),
            jax.ShapeDtypeStruct((m // seq, n_slabs, seq, LANES), BF16),
            jax.ShapeDtypeStruct((m // seq, nt, wv, tm), BF16),
            jax.ShapeDtypeStruct((m, wb), BF16),
        ],
        scratch_shapes=[
            pltpu.VMEM((tm, len(mlstm_chunks) * tn), BF16),
            pltpu.VMEM((2 * SUBLANES, tm), F32),
            pltpu.VMEM((N_HEADS_B, HEAD_DIM_B, HEAD_DIM_B), F32),
            pltpu.VMEM((N_HEADS_B, 1, HEAD_DIM_B), F32),
            pltpu.VMEM((SUBLANES, LANES), F32),
            pltpu.VMEM((SUBLANES, wb), F32),
            pltpu.VMEM((SUBLANES, wb), F32),
            pltpu.VMEM((tm // chunk, 2 * SUBLANES, chunk), F32),
        ],
        compiler_params=pltpu.CompilerParams(
            dimension_semantics=("arbitrary",), vmem_limit_bytes=VMEM_LIMIT),
        name="projmlstm",
    )(x2, g, w9, w_gg, w_gate, w_vt, jnp.broadcast_to(bias_row[:, None], (2 * SUBLANES, chunk)),
      conv_q, conv_k, norm_b)


def _attn_kernel(slopes_ref, qi_ref, kj_ref, lq_ref, q_ref, k_ref, vt_ref, za_ref, na_ref, o_ref,
                 qs_sc, s_sc, m_sc, acc_sc, *, tq, nq, lam_init):
    h = pl.program_id(1)
    d = HEAD_DIM_A
    dv = 2 * d
    slope2 = slopes_ref[h] * LOG2E
    rows2 = 2 * tq
    n_ct = rows2 // LANES
    n_steps = nq * (nq + 1) // 2

    lane = lax.broadcasted_iota(jnp.int32, (tq, LANES), 1)
    lane2 = lax.broadcasted_iota(jnp.int32, (rows2, LANES), 1)
    ones_cols = jnp.where(lane2 < 3, 1.0, 0.0).astype(BF16)
    for qb in range(nq):
        q = q_ref[qb * tq:(qb + 1) * tq, :]
        zero = jnp.zeros_like(q)
        qs = jnp.concatenate([jnp.where(lane < d, q, zero), jnp.where(lane >= d, q, zero)], axis=0)
        qs_sc[qb] = jnp.concatenate([qs, ones_cols], axis=1)
    group = 2
    n_groups = n_ct // group
    gw = group * LANES
    groups_per_map = tq // gw

    def make_k_bias(r0, n):
        pos = lax.broadcasted_iota(jnp.int32, (n, LANES), 0) + r0
        ln = lax.broadcasted_iota(jnp.int32, (n, LANES), 1)
        b_hi, b_mid, b_lo = (part.astype(F32) for part in _split3(pos.astype(F32) * slope2))
        bias = jnp.where(ln == 0, b_hi, jnp.where(ln == 1, b_mid, jnp.where(ln == 2, b_lo, 0.0)))
        return bias.astype(BF16)

    k_bias = make_k_bias(0, tq)
    k_bias_parts = [make_k_bias(r0, gw) for r0 in range(0, tq, gw)]
    ones_rows = {n: jnp.where(lax.broadcasted_iota(jnp.int32, (2 * SUBLANES, n), 0) == 0, 1.0, 0.0).astype(BF16)
                 for n in range(gw, tq + 1, gw)}

    lq = lq_ref[...]
    lam = (jnp.exp(jnp.sum(lq[0:1] * lq[1:2], axis=1, keepdims=True))
           - jnp.exp(jnp.sum(lq[2:3] * lq[3:4], axis=1, keepdims=True)) + lam_init)

    def scores(t, slot, parts=None):
        k = k_ref[pl.ds(pl.multiple_of(kj_ref[t] * tq, tq), tq), :]
        k_aug = jnp.concatenate([k, k_bias], axis=1)
        for g in (range(n_groups) if parts is None else parts):
            rows = slice(g * group * LANES, (g + 1) * group * LANES)
            s = lax.dot_general(k_aug, qs_sc[qi_ref[t], rows, :], (((1,), (1,)), ((), ())),
                                preferred_element_type=F32)
            for c in range(group):
                s_sc[slot, g * group + c] = s[:, c * LANES:(c + 1) * LANES]

    def diag_rows(g):
        return (g % groups_per_map + 1) * gw

    def scores_diag(t, slot):
        i = qi_ref[t]
        for r0 in range(0, tq, gw):
            k = k_ref[pl.ds(pl.multiple_of(i * tq + r0, gw), gw), :]
            k_aug = jnp.concatenate([k, k_bias_parts[r0 // gw]], axis=1)
            gs = [g for g in range(n_groups) if diag_rows(g) > r0]
            for g in gs:
                s = lax.dot_general(k_aug, qs_sc[i, g * gw:(g + 1) * gw, :],
                                    (((1,), (1,)), ((), ())), preferred_element_type=F32)
                for c in range(group):
                    s_sc[slot, g * group + c, r0:r0 + gw, :] = s[:, c * LANES:(c + 1) * LANES]

    def value_product(j, ps, rows=tq):
        vt_aug = jnp.concatenate([vt_ref[j, :, :rows], ones_rows[rows]], axis=0)
        return jnp.dot(vt_aug, jnp.concatenate(ps, axis=1), preferred_element_type=F32)

    def first_update(i, slot, between=None):
        for c0 in range(0, n_ct, group):
            rows = diag_rows(c0 // group)
            ps = []
            for c in range(c0, c0 + group):
                cs = slice(c * LANES, (c + 1) * LANES)
                kv = lax.broadcasted_iota(jnp.int32, (rows, LANES), 0)
                qv = lax.broadcasted_iota(jnp.int32, (rows, LANES), 1) + (c * LANES) % tq
                s = jnp.where(kv <= qv, s_sc[slot, c, :rows, :], NEG)
                m_new = jnp.max(s, axis=0, keepdims=True)
                ps.append(jnp.exp2(s - m_new).astype(BF16))
                m_sc[i, :, cs] = m_new
            acc_sc[i, :, c0 * LANES:(c0 + group) * LANES] = value_product(i, ps, rows)
            if between is not None:
                between(c0 // group)

    def update(i, j, slot, between=None):
        off = ((j - i) * tq).astype(F32) * slope2
        for c0 in range(0, n_ct, group):
            gs = slice(c0 * LANES, (c0 + group) * LANES)
            ps, alphas = [], []
            for c in range(c0, c0 + group):
                cs = slice(c * LANES, (c + 1) * LANES)
                s = s_sc[slot, c]
                m_prev = m_sc[i, :, cs]
                m_new = jnp.maximum(m_prev, jnp.max(s, axis=0, keepdims=True) + off)
                alphas.append(jnp.exp2(m_prev - m_new))
                ps.append(jnp.exp2(s - (m_new - off)).astype(BF16))
                m_sc[i, :, cs] = m_new
            acc_sc[i, :, gs] = jnp.concatenate(alphas, axis=1) * acc_sc[i, :, gs] + value_product(j, ps)
            if between is not None:
                between(c0 // group)

    def finalize(i, carry):
        rows = pl.ds(pl.multiple_of(i * tq, tq), tq)
        ot = acc_sc[i, :dv, :] / acc_sc[i, dv:dv + 1, :]
        od = (ot[:, :tq] - lam * ot[:, tq:]).T
        ms = jnp.mean(od * od, axis=-1, keepdims=True)
        y = od * lax.rsqrt(ms + EPS) * na_ref[...] * (1.0 - lam_init)
        za = za_ref[rows, :].astype(F32)
        o_ref[rows, :] = (y * (za * _sigmoid(za))).astype(BF16)
        return carry

    unroll = s_sc.shape[0]
    assert nq % unroll == 0
    scores_diag(0, 0)

    def next_scores(t, u, diag=False):
        def emit(g):
            if g == 0:
                (scores_diag if diag else scores)(t + 1, (u + 1) % unroll)
        return emit

    def diag_body(tt, carry):
        for u in range(unroll):
            t = unroll * tt + u
            first_update(qi_ref[t], u, between=next_scores(t, u, diag=u + 1 < unroll))
        return carry

    lax.fori_loop(0, nq // unroll, diag_body, 0)

    n_trips = (n_steps - nq) // unroll
    assert (n_steps - nq) % unroll == 0 and n_trips == nq - 1 and unroll >= (nq - 1) / 2

    def off_body(tt, carry):
        for u in range(unroll):
            t = nq + unroll * tt + u
            update(qi_ref[t], kj_ref[t], u, between=next_scores(t, u))
            if u == 0:
                finalize(tt, carry)
        return carry

    lax.fori_loop(0, n_trips, off_body, 0)
    finalize(jnp.int32(nq - 1), 0)


def _attention(pa, vt4, slopes, lq, norm_a, *, lam_init, tq=512):
    b, _, s, _ = pa.shape
    h = N_HEADS_A
    nq = s // tq
    nt = vt4.shape[1]
    assert vt4.shape[3] == tq and nt == nq and nq % 2 == 0 and (nq * (nq - 1) // 2) % 2 == 0
    pairs = ([(i, i) for i in range(nq)] + [(i, j) for i in range(nq) for j in range(i)] + [(0, 0)])
    qi = jnp.asarray([p[0] for p in pairs], jnp.int32)
    kj = jnp.asarray([p[1] for p in pairs], jnp.int32)
    slab = lambda base: (lambda bi, hi: (bi, base + hi, 0, 0))
    smem = pl.BlockSpec(memory_space=pltpu.SMEM)
    return pl.pallas_call(
        functools.partial(_attn_kernel, tq=tq, nq=nq, lam_init=lam_init),
        grid=(b, h),
        in_specs=[
            smem, smem, smem,
            pl.BlockSpec((4, HEAD_DIM_A), lambda bi, hi: (0, 0)),
            pl.BlockSpec((None, None, s, LANES), slab(0)),
            pl.BlockSpec((None, None, s, LANES), slab(h)),
            pl.BlockSpec((None, nt, LANES, tq), lambda bi, hi: (bi, 0, hi, 0)),
            pl.BlockSpec((None, None, s, LANES), slab(2 * h)),
            pl.BlockSpec((1, LANES), lambda bi, hi: (0, hi)),
        ],
        out_specs=pl.BlockSpec((None, None, s, LANES), lambda bi, hi: (bi, hi, 0, 0)),
        out_shape=jax.ShapeDtypeStruct((b, h, s, LANES), BF16),
        scratch_shapes=[
            pltpu.VMEM((nq, 2 * tq, 2 * LANES), BF16),
            pltpu.VMEM((4, 2 * tq // LANES, tq, LANES), F32),
            pltpu.VMEM((nq, 1, 2 * tq), F32),
            pltpu.VMEM((nq, LANES + 2 * SUBLANES, 2 * tq), F32),
        ],
        compiler_params=pltpu.CompilerParams(
            dimension_semantics=("arbitrary", "arbitrary"), vmem_limit_bytes=VMEM_LIMIT),
        name="diffattn",
    )(slopes, qi, kj, lq, pa, pa, vt4, pa, norm_a)


def _merge_kernel(x_ref, ya_ref, yb_ref, ga_ref, gb_ref, wa_ref, wb_ref, wo_ref, g_ref, o_ref):
    ya = jnp.concatenate([ya_ref[hd] for hd in range(ya_ref.shape[0])], axis=1)
    a = jnp.dot(ya, wa_ref[...], preferred_element_type=F32)
    b = jnp.dot(yb_ref[...], wb_ref[...], preferred_element_type=F32)
    merged = _sigmoid(ga_ref[...].astype(F32)) * a + _sigmoid(gb_ref[...].astype(F32)) * b
    out = jnp.dot(merged.astype(BF16), wo_ref[...], preferred_element_type=F32)
    ms = jnp.mean(out * out, axis=-1, keepdims=True)
    o_ref[...] = x_ref[...] + out * lax.rsqrt(ms + EPS) * g_ref[...]


def _merge(x2, ya4, yb2, pg, w_a, w_b, w_out, layer, g, *, tm=1024):
    m, d = x2.shape
    _, ha, seq, _ = ya4.shape
    wa = yb2.shape[1]
    nt = seq // tm
    row = lambda i: (i, 0)
    const = lambda i: (0, 0)
    return pl.pallas_call(
        _merge_kernel,
        grid=(m // tm,),
        in_specs=[
            pl.BlockSpec((tm, d), row),
            pl.BlockSpec((None, ha, tm, LANES), lambda i: (i // nt, 0, i % nt, 0)),
            pl.BlockSpec((tm, wa), row),
            pl.BlockSpec((tm, d), lambda i: (i, 0)),
            pl.BlockSpec((tm, d), lambda i: (i, 1)),
            pl.BlockSpec((None, wa, d), lambda i: (layer, 0, 0)),
            pl.BlockSpec((None, wa, d), lambda i: (layer, 0, 0)),
            pl.BlockSpec((None, d, d), lambda i: (layer, 0, 0)),
            pl.BlockSpec((1, d), const),
        ],
        out_specs=pl.BlockSpec((tm, d), row),
        out_shape=jax.ShapeDtypeStruct((m, d), F32),
        compiler_params=pltpu.CompilerParams(
            dimension_semantics=("arbitrary",), vmem_limit_bytes=VMEM_LIMIT),
        name="merge",
    )(x2, ya4, yb2, pg, pg, w_a, w_b, w_out, g)


def _wprep_main_kernel(wt_ref, w_ref, wvt_ref, *, va_chunk, qa_chunk, qa_scale):
    scale = jnp.where(pl.program_id(1) == qa_chunk, qa_scale, 1.0).astype(F32)
    w_ref[...] = (wt_ref[...] * scale).T.astype(BF16)

    @pl.when(pl.program_id(1) == va_chunk)
    def _():
        wvt_ref[...] = wt_ref[...].astype(BF16)


def _wprep_tail_kernel(lo_ref, hi_ref, w_ref, gate_ref, *, ng):
    tn = w_ref.shape[1]
    both = jnp.concatenate([lo_ref[...], hi_ref[:SUBLANES, :]], axis=0)
    w_ref[...] = both[ng:ng + tn].T.astype(BF16)

    @pl.when(pl.program_id(1) == 0)
    def _():
        head = lo_ref[:2 * SUBLANES, :]
        row = lax.broadcasted_iota(jnp.int32, head.shape, 0)
        gate_ref[...] = jnp.where(row < ng, head, 0.0).astype(BF16)


def _prep_weights(w_in, n9, ng, va_start, qa_width, qa_scale, *, tn=512):
    depth, d, n_in = w_in.shape
    ngg = n_in - n9 - ng
    assert n9 % tn == 0 and ngg % tn == 0 and ng == SUBLANES and va_start % tn == 0 and qa_width == tn
    w_t = jnp.transpose(w_in, (0, 2, 1))
    params = pltpu.CompilerParams(dimension_semantics=("arbitrary", "arbitrary"),
                                  vmem_limit_bytes=VMEM_LIMIT)
    w9, w_vt = pl.pallas_call(
        functools.partial(_wprep_main_kernel, va_chunk=va_start // tn, qa_chunk=0, qa_scale=qa_scale),
        grid=(depth, n9 // tn),
        in_specs=[pl.BlockSpec((None, tn, d), lambda l, c: (l, c, 0))],
        out_specs=[pl.BlockSpec((None, d, tn), lambda l, c: (l, 0, c)),
                   pl.BlockSpec((None, tn, d), lambda l, c: (l, 0, 0))],
        out_shape=[jax.ShapeDtypeStruct((depth, d, n9), BF16),
                   jax.ShapeDtypeStruct((depth, tn, d), BF16)],
        compiler_params=params, name="wprep_main",
    )(w_t)
    first = n9 // tn
    w_gg, w_gate = pl.pallas_call(
        functools.partial(_wprep_tail_kernel, ng=ng),
        grid=(depth, ngg // tn),
        in_specs=[pl.BlockSpec((None, tn, d), lambda l, c: (l, first + c, 0)),
                  pl.BlockSpec((None, tn, d), lambda l, c: (l, first + c + 1, 0))],
        out_specs=[pl.BlockSpec((None, d, tn), lambda l, c: (l, 0, c)),
                   pl.BlockSpec((None, 2 * SUBLANES, d), lambda l, c: (l, 0, 0))],
        out_shape=[jax.ShapeDtypeStruct((depth, d, ngg), BF16),
                   jax.ShapeDtypeStruct((depth, 2 * SUBLANES, d), BF16)],
        compiler_params=params, name="wprep_tail",
    )(w_t, w_t)
    return w9, w_gg, w_gate, w_vt


def kernel(x, norm_pre, norm_post, w_in, b_if, conv_qk, lambda_qk, norm_a, norm_b, w_a, w_b, w_out):
    bsz, seq, d = x.shape
    depth = w_in.shape[0]
    wa = N_HEADS_A * 2 * HEAD_DIM_A
    wb = N_HEADS_B * HEAD_DIM_B
    n9 = 4 * wa + 5 * wb
    ng = 2 * N_HEADS_B
    assert w_in.shape[2] == n9 + ng + 2 * d and d == 1024 and wa == 512 and wb == 512
    slopes = jnp.asarray(2.0 ** (-8.0 * np.arange(1, N_HEADS_A + 1) / N_HEADS_A), F32)
    w9, w_gg, w_gate, w_vt = _prep_weights(w_in, n9, ng, va_start=2 * wa, qa_width=wa,
                                           qa_scale=HEAD_DIM_A ** -0.5 * LOG2E)
    w_a_b, w_b_b, w_out_b = w_a.astype(BF16), w_b.astype(BF16), w_out.astype(BF16)

    x2 = x.reshape(bsz * seq, d)
    for l in range(depth):
        bias_row = jnp.pad(b_if[l], (0, 2 * SUBLANES - ng))
        lam_init = 0.8 - 0.6 * math.exp(-0.3 * l)
        pg, pa, vt4, yb2 = _proj_mlstm(x2, norm_pre[l].reshape(1, d), w9, w_gg, w_gate, w_vt, l, bias_row,
                                   conv_qk[l][:, :wb], conv_qk[l][:, wb:], norm_b[l].reshape(1, wb),
                                   seq=seq)
        ya = _attention(pa, vt4, slopes, lambda_qk[l], norm_a[l].reshape(1, wa), lam_init=lam_init)
        x2 = _merge(x2, ya, yb2, pg,
                    w_a_b, w_b_b, w_out_b, l, norm_post[l].reshape(1, d))
    return x2.reshape(bsz, seq, d)
```

```python
import functools
import math

import jax
import jax.numpy as jnp
import numpy as np
from jax import lax
from jax.experimental import pallas as pl
from jax.experimental.pallas import tpu as pltpu

F32 = jnp.float32
BF16 = jnp.bfloat16

N_HEADS_A = 4
HEAD_DIM_A = 64
N_HEADS_B = 4
HEAD_DIM_B = 128
CONV_W = 4
EPS = 1e-6

LANES = 128
SUBLANES = 8
NEG = -1e30
LOG2E = math.log2(math.e)
VMEM_LIMIT = 56 * 1024 * 1024


def _sigmoid(x):
    return jax.nn.sigmoid(x)


def _split3(x):
    hi = x.astype(BF16)
    r1 = x - hi.astype(F32)
    mid = r1.astype(BF16)
    lo = (r1 - mid.astype(F32)).astype(BF16)
    return hi, mid, lo


def _shift_rows(x, tail, sh):
    r = pltpu.roll(x, sh, 0)
    rt = pltpu.roll(tail, sh, 0)
    row = lax.broadcasted_iota(jnp.int32, tail.shape, 0)
    first = jnp.where(row < sh, rt, r[:SUBLANES])
    return jnp.concatenate([first, r[SUBLANES:]], axis=0)


def _conv_silu(xb, tail_sc, w_ref):
    x = xb.astype(F32)
    tail = tail_sc[...]
    w = w_ref[...]
    y = x * w[CONV_W - 1:CONV_W]
    for tap in range(CONV_W - 1):
        y = y + _shift_rows(x, tail, CONV_W - 1 - tap) * w[tap:tap + 1]
    tail_sc[...] = x[x.shape[0] - SUBLANES:]
    return y * _sigmoid(y)


def _mlstm_gates(r0, L, pm_sc, g_sc, bias_ref, cq_ref, ck_ref, tq_sc, tk_sc, wt_sc):
    nh, dh = N_HEADS_B, HEAD_DIM_B
    wb = nh * dh
    rows = slice(r0, r0 + L)
    q_all = _conv_silu(pm_sc[rows, 0:wb], tq_sc, cq_ref)
    k_all = _conv_silu(pm_sc[rows, wb:2 * wb], tk_sc, ck_ref) * (dh ** -0.5)

    g_t = g_sc[:, rows] + bias_ref[...]
    lf_t = jnp.minimum(g_t, 0.0) - jnp.log1p(jnp.exp(-jnp.abs(g_t)))
    rr = lax.broadcasted_iota(jnp.int32, (L, L), 0)
    cc = lax.broadcasted_iota(jnp.int32, (L, L), 1)
    causal = cc <= rr
    tri_t = jnp.where(rr <= cc, 1.0, 0.0).astype(BF16)
    cs_t = sum(jnp.dot(part, tri_t, preferred_element_type=F32) for part in _split3(lf_t))
    gate_row = lax.broadcasted_iota(jnp.int32, g_t.shape, 0)
    w_t = jnp.where(gate_row < nh, g_t, cs_t)
    wt_sc[...] = w_t
    w = jnp.concatenate([w_t, jnp.zeros((LANES - w_t.shape[0], L), F32)], axis=0).T
    lane = lax.broadcasted_iota(jnp.int32, (L, LANES), 1)
    return q_all, k_all, w, causal, lane


def _mlstm_head(h, r0, L, gates, pm_sc, nb_ref, y_ref, c_sc, n_sc, m_sc, wt_sc):
    nh, dh = N_HEADS_B, HEAD_DIM_B
    wb = nh * dh
    rows = slice(r0, r0 + L)
    hs = slice(h * dh, (h + 1) * dh)
    q_all, k_all, w, causal, lane = gates
    w_last = w[L - 1:L, :]
    lane1 = lax.broadcasted_iota(jnp.int32, (1, LANES), 1)

    def pick(x, ln, idx):
        return jnp.sum(jnp.where(ln == idx, x, 0.0), axis=1, keepdims=True)

    b_col = pick(w, lane, nh + h)
    i_col = pick(w, lane, h)
    b_row = wt_sc[nh + h:nh + h + 1, :]
    i_row = wt_sc[h:h + 1, :]
    b_last = pick(w_last, lane1, nh + h)
    m_prev = jnp.max(m_sc[h:h + 1, :], axis=1, keepdims=True)

    log_d = jnp.where(causal, b_col + (i_row - b_row), NEG)
    inter = b_col + m_prev
    m_row = jnp.maximum(inter, jnp.max(log_d, axis=1, keepdims=True))
    d_m = jnp.exp(log_d - m_row)
    w_inter = jnp.exp(inter - m_row)

    q = q_all[:, hs]
    k = k_all[:, hs]
    qb = q.astype(BF16)
    v = pm_sc[rows, 2 * wb + h * dh:2 * wb + (h + 1) * dh]
    s_m = lax.dot_general(qb, k.astype(BF16), (((1,), (1,)), ((), ())),
                          preferred_element_type=F32) * d_m
    c_old = c_sc[h]
    n_old = n_sc[h]
    num = (jnp.dot(s_m.astype(BF16), v, preferred_element_type=F32)
           + w_inter * jnp.dot(qb, c_old.astype(BF16), preferred_element_type=F32))
    den = (jnp.sum(s_m, axis=1, keepdims=True)
           + w_inter * jnp.sum(q * n_old, axis=1, keepdims=True))
    hval = num / jnp.maximum(jnp.abs(den), jnp.exp(-m_row))

    a_col = b_last - b_col + i_col
    m_new = jnp.maximum(b_last + m_prev, jnp.max(a_col, axis=0, keepdims=True))
    decay = jnp.exp(b_last + m_prev - m_new)
    kw = k * jnp.exp(a_col - m_new)
    c_sc[h] = decay * c_old + jnp.dot(kw.T.astype(BF16), v, preferred_element_type=F32)
    n_sc[h] = decay * n_old + jnp.sum(kw, axis=0, keepdims=True)
    m_sc[h:h + 1, :] = jnp.broadcast_to(m_new, (1, LANES))

    o = pm_sc[rows, 3 * wb + h * dh:3 * wb + (h + 1) * dh].astype(F32)
    hb = _sigmoid(o) * hval
    ms = jnp.mean(hb * hb, axis=-1, keepdims=True)
    yb = hb * lax.rsqrt(ms + EPS) * nb_ref[:, hs]
    z = pm_sc[rows, 4 * wb + h * dh:4 * wb + (h + 1) * dh].astype(F32)
    y_ref[rows, hs] = (yb * (z * _sigmoid(z))).astype(BF16)


def _proj_kernel(x_ref, g_ref, w9_ref, wgg_ref, wg_ref, wvt_ref, bias_ref, cq_ref, ck_ref, nb_ref,
                 pg_ref, pa_ref, vt_ref, y_ref, pm_sc, g_sc, c_sc, n_sc, m_sc, tq_sc, tk_sc, wt_sc,
                 *, tn, chunk, steps_per_seq, attn_chunks, mlstm_chunks):
    @pl.when(pl.program_id(0) % steps_per_seq == 0)
    def _():
        c_sc[...] = jnp.zeros(c_sc.shape, F32)
        n_sc[...] = jnp.zeros(n_sc.shape, F32)
        m_sc[...] = jnp.zeros(m_sc.shape, F32)
        tq_sc[...] = jnp.zeros(tq_sc.shape, F32)
        tk_sc[...] = jnp.zeros(tk_sc.shape, F32)

    x = x_ref[...]
    ms = jnp.mean(x * x, axis=-1, keepdims=True)
    h = (x * lax.rsqrt(ms + EPS) * g_ref[...]).astype(BF16)
    n_chunks = x.shape[0] // chunk
    n_gg = wgg_ref.shape[1] // tn

    def project(w_ref, src):
        return jnp.dot(h, w_ref[:, src * tn:(src + 1) * tn], preferred_element_type=F32).astype(BF16)

    def to_pm(c):
        pm_sc[:, c * tn:(c + 1) * tn] = project(w9_ref, mlstm_chunks[c])

    def to_p_gg(c):
        pg_ref[:, c * tn:(c + 1) * tn] = project(wgg_ref, c)

    def to_p_attn(c):
        res = project(w9_ref, attn_chunks[c])
        per = tn // LANES
        for hd in range(per):
            pa_ref[c * per + hd] = res[:, hd * LANES:(hd + 1) * LANES]

    def to_vt():
        vt_ref[...] = lax.dot_general(wvt_ref[...], h, (((1,), (1,)), ((), ())),
                                      preferred_element_type=F32).astype(BF16)

    fillers = ([functools.partial(to_p_gg, c) for c in range(n_gg)]
               + [functools.partial(to_p_attn, c) for c in range(len(attn_chunks))] + [to_vt])
    to_pm(0)
    to_pm(1)
    g_sc[...] = lax.dot_general(wg_ref[...], h, (((1,), (1,)), ((), ())), preferred_element_type=F32)
    gates = []
    for ci in range(n_chunks):
        if 2 + ci < len(mlstm_chunks):
            to_pm(2 + ci)
        gates.append(_mlstm_gates(ci * chunk, chunk, pm_sc, g_sc, bias_ref, cq_ref, ck_ref,
                                  tq_sc, tk_sc, wt_sc.at[ci]))
    for c in range(2 + n_chunks, len(mlstm_chunks)):
        to_pm(c)
    for ci in range(n_chunks):
        for hd in range(N_HEADS_B):
            _mlstm_head(hd, ci * chunk, chunk, gates[ci], pm_sc, nb_ref, y_ref,
                        c_sc, n_sc, m_sc, wt_sc.at[ci])
            if fillers:
                fillers.pop(0)()
    for filler in fillers:
        filler()


def _proj_mlstm(x2, g, w9, w_gg, w_gate, w_vt, layer, bias_row, conv_q, conv_k, norm_b,
                *, seq, tm=512, tn=512, chunk=256):
    m, d = x2.shape
    n9 = w9.shape[2]
    ngg = w_gg.shape[2]
    wv = w_vt.shape[1]
    wb = N_HEADS_B * HEAD_DIM_B
    attn_chunks, mlstm_chunks = (0, 1, 3), (4, 5, 6, 7, 8)
    assert n9 == 9 * tn and wb == tn and wv == tn
    n_slabs = len(attn_chunks) * tn // LANES
    nt = seq // tm
    layer_blk = lambda i: (layer, 0, 0)
    const = lambda i: (0, 0)
    once = dict(pipeline_mode=pl.Buffered(1))
    return pl.pallas_call(
        functools.partial(_proj_kernel, tn=tn, chunk=chunk, steps_per_seq=nt,
                          attn_chunks=attn_chunks, mlstm_chunks=mlstm_chunks),
        grid=(m // tm,),
        in_specs=[
            pl.BlockSpec((tm, d), lambda i: (i, 0)),
            pl.BlockSpec((1, d), const),
            pl.BlockSpec((None, d, n9), layer_blk, **once),
            pl.BlockSpec((None, d, ngg), layer_blk, **once),
            pl.BlockSpec((None, 2 * SUBLANES, d), layer_blk),
            pl.BlockSpec((None, wv, d), layer_blk),
            pl.BlockSpec((2 * SUBLANES, chunk), const),
            pl.BlockSpec((CONV_W, wb), const),
            pl.BlockSpec((CONV_W, wb), const),
            pl.BlockSpec((1, wb), const),
        ],
        out_specs=[
            pl.BlockSpec((tm, ngg), lambda i: (i, 0)),
            pl.BlockSpec((None, n_slabs, tm, LANES), lambda i: (i // nt, 0, i % nt, 0)),
            pl.BlockSpec((None, None, wv, tm), lambda i: (i // nt, i % nt, 0, 0)),
            pl.BlockSpec((tm, wb), lambda i: (i, 0)),
        ],
        out_shape=[
            jax.ShapeDtypeStruct((m, ngg), BF16),
            jax.ShapeDtypeStruct((m // seq, n_slabs, seq, LANES), BF16),
            jax.ShapeDtypeStruct((m // seq, nt, wv, tm), BF16),
            jax.ShapeDtypeStruct((m, wb), BF16),
        ],
        scratch_shapes=[
            pltpu.VMEM((tm, len(mlstm_chunks) * tn), BF16),
            pltpu.VMEM((2 * SUBLANES, tm), F32),
            pltpu.VMEM((N_HEADS_B, HEAD_DIM_B, HEAD_DIM_B), F32),
            pltpu.VMEM((N_HEADS_B, 1, HEAD_DIM_B), F32),
            pltpu.VMEM((SUBLANES, LANES), F32),
            pltpu.VMEM((SUBLANES, wb), F32),
            pltpu.VMEM((SUBLANES, wb), F32),
            pltpu.VMEM((tm // chunk, 2 * SUBLANES, chunk), F32),
        ],
        compiler_params=pltpu.CompilerParams(
            dimension_semantics=("arbitrary",), vmem_limit_bytes=VMEM_LIMIT),
        name="projmlstm",
    )(x2, g, w9, w_gg, w_gate, w_vt, jnp.broadcast_to(bias_row[:, None], (2 * SUBLANES, chunk)),
      conv_q, conv_k, norm_b)


def _attn_kernel(slopes_ref, qi_ref, kj_ref, lq_ref, q_ref, k_ref, vt_ref, za_ref, na_ref, o_ref,
                 qs_sc, s_sc, m_sc, acc_sc, *, tq, nq, lam_init):
    h = pl.program_id(1)
    d = HEAD_DIM_A
    dv = 2 * d
    slope2 = slopes_ref[h] * LOG2E
    rows2 = 2 * tq
    n_ct = rows2 // LANES
    n_steps = nq * (nq + 1) // 2

    lane = lax.broadcasted_iota(jnp.int32, (tq, LANES), 1)
    lane2 = lax.broadcasted_iota(jnp.int32, (rows2, LANES), 1)
    ones_cols = jnp.where(lane2 < 3, 1.0, 0.0).astype(BF16)
    for qb in range(nq):
        q = q_ref[qb * tq:(qb + 1) * tq, :]
        zero = jnp.zeros_like(q)
        qs = jnp.concatenate([jnp.where(lane < d, q, zero), jnp.where(lane >= d, q, zero)], axis=0)
        qs_sc[qb] = jnp.concatenate([qs, ones_cols], axis=1)
    group = 2
    n_groups = n_ct // group
    gw = group * LANES
    groups_per_map = tq // gw

    def make_k_bias(r0, n):
        pos = lax.broadcasted_iota(jnp.int32, (n, LANES), 0) + r0
        ln = lax.broadcasted_iota(jnp.int32, (n, LANES), 1)
        b_hi, b_mid, b_lo = (part.astype(F32) for part in _split3(pos.astype(F32) * slope2))
        bias = jnp.where(ln == 0, b_hi, jnp.where(ln == 1, b_mid, jnp.where(ln == 2, b_lo, 0.0)))
        return bias.astype(BF16)

    k_bias = make_k_bias(0, tq)
    k_bias_parts = [make_k_bias(r0, gw) for r0 in range(0, tq, gw)]
    ones_rows = {n: jnp.where(lax.broadcasted_iota(jnp.int32, (2 * SUBLANES, n), 0) == 0, 1.0, 0.0).astype(BF16)
                 for n in range(gw, tq + 1, gw)}

    lq = lq_ref[...]
    lam = (jnp.exp(jnp.sum(lq[0:1] * lq[1:2], axis=1, keepdims=True))
           - jnp.exp(jnp.sum(lq[2:3] * lq[3:4], axis=1, keepdims=True)) + lam_init)

    def scores(t, slot, parts=None):
        k = k_ref[pl.ds(pl.multiple_of(kj_ref[t] * tq, tq), tq), :]
        k_aug = jnp.concatenate([k, k_bias], axis=1)
        for g in (range(n_groups) if parts is None else parts):
            rows = slice(g * group * LANES, (g + 1) * group * LANES)
            s = lax.dot_general(k_aug, qs_sc[qi_ref[t], rows, :], (((1,), (1,)), ((), ())),
                                preferred_element_type=F32)
            for c in range(group):
                s_sc[slot, g * group + c] = s[:, c * LANES:(c + 1) * LANES]

    def diag_rows(g):
        return (g % groups_per_map + 1) * gw

    def scores_diag(t, slot):
        i = qi_ref[t]
        for r0 in range(0, tq, gw):
            k = k_ref[pl.ds(pl.multiple_of(i * tq + r0, gw), gw), :]
            k_aug = jnp.concatenate([k, k_bias_parts[r0 // gw]], axis=1)
            gs = [g for g in range(n_groups) if diag_rows(g) > r0]
            for g in gs:
                s = lax.dot_general(k_aug, qs_sc[i, g * gw:(g + 1) * gw, :],
                                    (((1,), (1,)), ((), ())), preferred_element_type=F32)
                for c in range(group):
                    s_sc[slot, g * group + c, r0:r0 + gw, :] = s[:, c * LANES:(c + 1) * LANES]

    def value_product(j, ps, rows=tq):
        vt_aug = jnp.concatenate([vt_ref[j, :, :rows], ones_rows[rows]], axis=0)
        return jnp.dot(vt_aug, jnp.concatenate(ps, axis=1), preferred_element_type=F32)

    def first_update(i, slot, between=None):
        for c0 in range(0, n_ct, group):
            rows = diag_rows(c0 // group)
            ps = []
            for c in range(c0, c0 + group):
                cs = slice(c * LANES, (c + 1) * LANES)
                kv = lax.broadcasted_iota(jnp.int32, (rows, LANES), 0)
                qv = lax.broadcasted_iota(jnp.int32, (rows, LANES), 1) + (c * LANES) % tq
                s = jnp.where(kv <= qv, s_sc[slot, c, :rows, :], NEG)
                m_new = jnp.max(s, axis=0, keepdims=True)
                ps.append(jnp.exp2(s - m_new).astype(BF16))
                m_sc[i, :, cs] = m_new
            acc_sc[i, :, c0 * LANES:(c0 + group) * LANES] = value_product(i, ps, rows)
            if between is not None:
                between(c0 // group)

    def update(i, j, slot, between=None):
        off = ((j - i) * tq).astype(F32) * slope2
        for c0 in range(0, n_ct, group):
            gs = slice(c0 * LANES, (c0 + group) * LANES)
            ps, alphas = [], []
            for c in range(c0, c0 + group):
                cs = slice(c * LANES, (c + 1) * LANES)
                s = s_sc[slot, c]
                m_prev = m_sc[i, :, cs]
                m_new = jnp.maximum(m_prev, jnp.max(s, axis=0, keepdims=True) + off)
                alphas.append(jnp.exp2(m_prev - m_new))
                ps.append(jnp.exp2(s - (m_new - off)).astype(BF16))
                m_sc[i, :, cs] = m_new
            acc_sc[i, :, gs] = jnp.concatenate(alphas, axis=1) * acc_sc[i, :, gs] + value_product(j, ps)
            if between is not None:
                between(c0 // group)

    def finalize(i, carry):
        rows = pl.ds(pl.multiple_of(i * tq, tq), tq)
        ot = acc_sc[i, :dv, :] / acc_sc[i, dv:dv + 1, :]
        od = (ot[:, :tq] - lam * ot[:, tq:]).T
        ms = jnp.mean(od * od, axis=-1, keepdims=True)
        y = od * lax.rsqrt(ms + EPS) * na_ref[...] * (1.0 - lam_init)
        za = za_ref[rows, :].astype(F32)
        o_ref[rows, :] = (y * (za * _sigmoid(za))).astype(BF16)
        return carry

    unroll = s_sc.shape[0]
    assert nq % unroll == 0
    scores_diag(0, 0)

    def next_scores(t, u, diag=False):
        def emit(g):
            if g == 0:
                (scores_diag if diag else scores)(t + 1, (u + 1) % unroll)
        return emit

    def diag_body(tt, carry):
        for u in range(unroll):
            t = unroll * tt + u
            first_update(qi_ref[t], u, between=next_scores(t, u, diag=u + 1 < unroll))
        return carry

    lax.fori_loop(0, nq // unroll, diag_body, 0)

    n_trips = (n_steps - nq) // unroll
    assert (n_steps - nq) % unroll == 0 and n_trips == nq - 1 and unroll >= (nq - 1) / 2

    def off_body(tt, carry):
        for u in range(unroll):
            t = nq + unroll * tt + u
            update(qi_ref[t], kj_ref[t], u, between=next_scores(t, u))
            if u == 0:
                finalize(tt, carry)
        return carry

    lax.fori_loop(0, n_trips, off_body, 0)
    finalize(jnp.int32(nq - 1), 0)


def _attention(pa, vt4, slopes, lq, norm_a, *, lam_init, tq=512):
    b, _, s, _ = pa.shape
    h = N_HEADS_A
    nq = s // tq
    nt = vt4.shape[1]
    assert vt4.shape[3] == tq and nt == nq and nq % 2 == 0 and (nq * (nq - 1) // 2) % 2 == 0
    pairs = ([(i, i) for i in range(nq)] + [(i, j) for i in range(nq) for j in range(i)] + [(0, 0)])
    qi = jnp.asarray([p[0] for p in pairs], jnp.int32)
    kj = jnp.asarray([p[1] for p in pairs], jnp.int32)
    slab = lambda base: (lambda bi, hi: (bi, base + hi, 0, 0))
    smem = pl.BlockSpec(memory_space=pltpu.SMEM)
    return pl.pallas_call(
        functools.partial(_attn_kernel, tq=tq, nq=nq, lam_init=lam_init),
        grid=(b, h),
        in_specs=[
            smem, smem, smem,
            pl.BlockSpec((4, HEAD_DIM_A), lambda bi, hi: (0, 0)),
            pl.BlockSpec((None, None, s, LANES), slab(0)),
            pl.BlockSpec((None, None, s, LANES), slab(h)),
            pl.BlockSpec((None, nt, LANES, tq), lambda bi, hi: (bi, 0, hi, 0)),
            pl.BlockSpec((None, None, s, LANES), slab(2 * h)),
            pl.BlockSpec((1, LANES), lambda bi, hi: (0, hi)),
        ],
        out_specs=pl.BlockSpec((None, None, s, LANES), lambda bi, hi: (bi, hi, 0, 0)),
        out_shape=jax.ShapeDtypeStruct((b, h, s, LANES), BF16),
        scratch_shapes=[
            pltpu.VMEM((nq, 2 * tq, 2 * LANES), BF16),
            pltpu.VMEM((4, 2 * tq // LANES, tq, LANES), F32),
            pltpu.VMEM((nq, 1, 2 * tq), F32),
            pltpu.VMEM((nq, LANES + 2 * SUBLANES, 2 * tq), F32),
        ],
        compiler_params=pltpu.CompilerParams(
            dimension_semantics=("arbitrary", "arbitrary"), vmem_limit_bytes=VMEM_LIMIT),
        name="diffattn",
    )(slopes, qi, kj, lq, pa, pa, vt4, pa, norm_a)


def _merge_kernel(x_ref, ya_ref, yb_ref, ga_ref, gb_ref, wa_ref, wb_ref, wo_ref, g_ref, o_ref, *, sub):
    for r0 in range(0, x_ref.shape[0], sub):
        rows = slice(r0, r0 + sub)
        ya = jnp.concatenate([ya_ref[hd, rows, :] for hd in range(ya_ref.shape[0])], axis=1)
        a = jnp.dot(ya, wa_ref[...], preferred_element_type=F32)
        b = jnp.dot(yb_ref[rows, :], wb_ref[...], preferred_element_type=F32)
        merged = _sigmoid(ga_ref[rows, :].astype(F32)) * a + _sigmoid(gb_ref[rows, :].astype(F32)) * b
        out = jnp.dot(merged.astype(BF16), wo_ref[...], preferred_element_type=F32)
        ms = jnp.mean(out * out, axis=-1, keepdims=True)
        o_ref[rows, :] = x_ref[rows, :] + out * lax.rsqrt(ms + EPS) * g_ref[...]


def _merge(x2, ya4, yb2, pg, w_a, w_b, w_out, layer, g, *, tm=1024):
    m, d = x2.shape
    _, ha, seq, _ = ya4.shape
    wa = yb2.shape[1]
    nt = seq // tm
    row = lambda i: (i, 0)
    const = lambda i: (0, 0)
    return pl.pallas_call(
        functools.partial(_merge_kernel, sub=tm // 2),
        grid=(m // tm,),
        in_specs=[
            pl.BlockSpec((tm, d), row),
            pl.BlockSpec((None, ha, tm, LANES), lambda i: (i // nt, 0, i % nt, 0)),
            pl.BlockSpec((tm, wa), row),
            pl.BlockSpec((tm, d), lambda i: (i, 0)),
            pl.BlockSpec((tm, d), lambda i: (i, 1)),
            pl.BlockSpec((None, wa, d), lambda i: (layer, 0, 0)),
            pl.BlockSpec((None, wa, d), lambda i: (layer, 0, 0)),
            pl.BlockSpec((None, d, d), lambda i: (layer, 0, 0)),
            pl.BlockSpec((1, d), const),
        ],
        out_specs=pl.BlockSpec((tm, d), row),
        out_shape=jax.ShapeDtypeStruct((m, d), F32),
        compiler_params=pltpu.CompilerParams(
            dimension_semantics=("arbitrary",), vmem_limit_bytes=VMEM_LIMIT),
        name="merge",
    )(x2, ya4, yb2, pg, pg, w_a, w_b, w_out, g)


def _wprep_main_kernel(wt_ref, w_ref, wvt_ref, *, va_chunk, qa_chunk, qa_scale):
    scale = jnp.where(pl.program_id(1) == qa_chunk, qa_scale, 1.0).astype(F32)
    w_ref[...] = (wt_ref[...] * scale).T.astype(BF16)

    @pl.when(pl.program_id(1) == va_chunk)
    def _():
        wvt_ref[...] = wt_ref[...].astype(BF16)


def _wprep_tail_kernel(lo_ref, hi_ref, w_ref, gate_ref, *, ng):
    tn = w_ref.shape[1]
    both = jnp.concatenate([lo_ref[...], hi_ref[:SUBLANES, :]], axis=0)
    w_ref[...] = both[ng:ng + tn].T.astype(BF16)

    @pl.when(pl.program_id(1) == 0)
    def _():
        head = lo_ref[:2 * SUBLANES, :]
        row = lax.broadcasted_iota(jnp.int32, head.shape, 0)
        gate_ref[...] = jnp.where(row < ng, head, 0.0).astype(BF16)


def _prep_weights(w_in, n9, ng, va_start, qa_width, qa_scale, *, tn=512):
    depth, d, n_in = w_in.shape
    ngg = n_in - n9 - ng
    assert n9 % tn == 0 and ngg % tn == 0 and ng == SUBLANES and va_start % tn == 0 and qa_width == tn
    w_t = jnp.transpose(w_in, (0, 2, 1))
    params = pltpu.CompilerParams(dimension_semantics=("arbitrary", "arbitrary"),
                                  vmem_limit_bytes=VMEM_LIMIT)
    w9, w_vt = pl.pallas_call(
        functools.partial(_wprep_main_kernel, va_chunk=va_start // tn, qa_chunk=0, qa_scale=qa_scale),
        grid=(depth, n9 // tn),
        in_specs=[pl.BlockSpec((None, tn, d), lambda l, c: (l, c, 0))],
        out_specs=[pl.BlockSpec((None, d, tn), lambda l, c: (l, 0, c)),
                   pl.BlockSpec((None, tn, d), lambda l, c: (l, 0, 0))],
        out_shape=[jax.ShapeDtypeStruct((depth, d, n9), BF16),
                   jax.ShapeDtypeStruct((depth, tn, d), BF16)],
        compiler_params=params, name="wprep_main",
    )(w_t)
    first = n9 // tn
    w_gg, w_gate = pl.pallas_call(
        functools.partial(_wprep_tail_kernel, ng=ng),
        grid=(depth, ngg // tn),
        in_specs=[pl.BlockSpec((None, tn, d), lambda l, c: (l, first + c, 0)),
                  pl.BlockSpec((None, tn, d), lambda l, c: (l, first + c + 1, 0))],
        out_specs=[pl.BlockSpec((None, d, tn), lambda l, c: (l, 0, c)),
                   pl.BlockSpec((None, 2 * SUBLANES, d), lambda l, c: (l, 0, 0))],
        out_shape=[jax.ShapeDtypeStruct((depth, d, ngg), BF16),
                   jax.ShapeDtypeStruct((depth, 2 * SUBLANES, d), BF16)],
        compiler_params=params, name="wprep_tail",
    )(w_t, w_t)
    return w9, w_gg, w_gate, w_vt


def kernel(x, norm_pre, norm_post, w_in, b_if, conv_qk, lambda_qk, norm_a, norm_b, w_a, w_b, w_out):
    bsz, seq, d = x.shape
    depth = w_in.shape[0]
    wa = N_HEADS_A * 2 * HEAD_DIM_A
    wb = N_HEADS_B * HEAD_DIM_B
    n9 = 4 * wa + 5 * wb
    ng = 2 * N_HEADS_B
    assert w_in.shape[2] == n9 + ng + 2 * d and d == 1024 and wa == 512 and wb == 512
    slopes = jnp.asarray(2.0 ** (-8.0 * np.arange(1, N_HEADS_A + 1) / N_HEADS_A), F32)
    w9, w_gg, w_gate, w_vt = _prep_weights(w_in, n9, ng, va_start=2 * wa, qa_width=wa,
                                           qa_scale=HEAD_DIM_A ** -0.5 * LOG2E)
    w_a_b, w_b_b, w_out_b = w_a.astype(BF16), w_b.astype(BF16), w_out.astype(BF16)

    x2 = x.reshape(bsz * seq, d)
    for l in range(depth):
        bias_row = jnp.pad(b_if[l], (0, 2 * SUBLANES - ng))
        lam_init = 0.8 - 0.6 * math.exp(-0.3 * l)
        pg, pa, vt4, yb2 = _proj_mlstm(x2, norm_pre[l].reshape(1, d), w9, w_gg, w_gate, w_vt, l, bias_row,
                                   conv_qk[l][:, :wb], conv_qk[l][:, wb:], norm_b[l].reshape(1, wb),
                                   seq=seq)
        ya = _attention(pa, vt4, slopes, lambda_qk[l], norm_a[l].reshape(1, wa), lam_init=lam_init)
        x2 = _merge(x2, ya, yb2, pg,
                    w_a_b, w_b_b, w_out_b, l, norm_post[l].reshape(1, d))
    return x2.reshape(bsz, seq, d)
```
